```python
import jax, jax.numpy as jnp
from jax import lax
import numpy as np

D_MODEL = 2048
BATCH = 1
SEQ = 8192
DEPTH = 1
DEC_BATCH = 1
DEC_SEQ = 16384
PAST_LEN = 128

GRID_W = 64
HEAD_DIM = 128
NA_HEADS = 8
NA_WIDTH = NA_HEADS * HEAD_DIM
NA_KH_MAX = 8
NA_KW = 16
MLA_HEADS = 8
Q_LORA_RANK = 512
KV_LORA_RANK = 512
QK_NOPE_DIM = 128
QK_ROPE_DIM = 64
V_HEAD_DIM = 128
MLA_QK_DIM = QK_NOPE_DIM + QK_ROPE_DIM
MLA_WIDTH = MLA_HEADS * V_HEAD_DIM
MIX_WIDTH = NA_WIDTH + MLA_WIDTH
IN_PROJ_WIDTH = 3 * NA_WIDTH + Q_LORA_RANK + KV_LORA_RANK + QK_ROPE_DIM
D_FF = -(-8 * D_MODEL // (3 * 256)) * 256
ROPE_THETA = 10000.0
RMS_EPS = 1e-6
Q_BLOCK = 128

kernel_name = "hybrid_natten_mla_encoder"


def _rmsnorm(x, g):
    xf = x.astype(jnp.float32)
    y = xf * lax.rsqrt(jnp.mean(xf * xf, axis=-1, keepdims=True) + RMS_EPS)
    return (y * g.astype(jnp.float32)).astype(x.dtype)


def _rotate_half(x):
    x1, x2 = jnp.split(x, 2, axis=-1)
    return jnp.concatenate([-x2, x1], axis=-1)


def _rope(x, seq_len):
    r = x.shape[-1]
    pos = jnp.arange(seq_len, dtype=jnp.float32)
    inv_freq = ROPE_THETA ** (-jnp.arange(0, r, 2, dtype=jnp.float32) / r)
    ang = pos[:, None] * inv_freq[None, :]
    ang = jnp.concatenate([ang, ang], axis=-1)[None, :, None, :]
    cos = jnp.cos(ang).astype(x.dtype)
    sin = jnp.sin(ang).astype(x.dtype)
    return x * cos + _rotate_half(x) * sin


def _neighbourhood_attention(q, k, v, rpb):
    B, S, H, dh = q.shape
    rows = S // GRID_W
    kh = min(NA_KH_MAX, rows)
    q = q.reshape(B, rows, GRID_W, H, dh)
    k = k.reshape(B, rows, GRID_W, H, dh)
    v = v.reshape(B, rows, GRID_W, H, dh)
    cols = jnp.arange(GRID_W)
    col_start = jnp.clip(cols - NA_KW // 2, 0, GRID_W - NA_KW)
    col_idx = col_start[:, None] + jnp.arange(NA_KW)[None, :]
    dc = col_idx - cols[:, None] + (NA_KW - 1)
    scale = HEAD_DIM ** -0.5

    def row_step(args):
        r, q_row = args
        rs = jnp.clip(r - kh // 2, 0, rows - kh)
        k_blk = lax.dynamic_slice_in_dim(k, rs, kh, axis=1)
        v_blk = lax.dynamic_slice_in_dim(v, rs, kh, axis=1)
        k_g = k_blk[:, :, col_idx]
        v_g = v_blk[:, :, col_idx]
        dr = rs + jnp.arange(kh) - r + (NA_KH_MAX - 1)
        bias = rpb[:, dr[None, :, None], dc[:, None, :]]
        s = jnp.einsum('bqhd,bkqwhd->bhqkw', q_row, k_g,
                       preferred_element_type=jnp.float32) * scale
        s = s + bias.astype(jnp.float32)[None]
        p = jax.nn.softmax(s.reshape(B, H, GRID_W, kh * NA_KW), axis=-1)
        p = p.reshape(B, H, GRID_W, kh, NA_KW).astype(v.dtype)
        return jnp.einsum('bhqkw,bkqwhd->bqhd', p, v_g)

    out = lax.map(row_step, (jnp.arange(rows), jnp.moveaxis(q, 1, 0)))
    return jnp.moveaxis(out, 0, 1).reshape(B, S, H * dh)


def _mla(q_c, kv_c, k_pe, g_q, w_uq, g_kv, w_ukv):
    B, S, _ = q_c.shape
    q = (_rmsnorm(q_c, g_q) @ w_uq).reshape(B, S, MLA_HEADS, MLA_QK_DIM)
    kv = (_rmsnorm(kv_c, g_kv) @ w_ukv).reshape(B, S, MLA_HEADS, QK_NOPE_DIM + V_HEAD_DIM)
    q_nope, q_pe = q[..., :QK_NOPE_DIM], q[..., QK_NOPE_DIM:]
    k_nope, v = kv[..., :QK_NOPE_DIM], kv[..., QK_NOPE_DIM:]
    q_pe = _rope(q_pe, S)
    k_pe = _rope(k_pe[:, :, None, :], S)
    q = jnp.concatenate([q_nope, q_pe], axis=-1)
    k = jnp.concatenate([k_nope, jnp.broadcast_to(k_pe, (B, S, MLA_HEADS, QK_ROPE_DIM))], axis=-1)
    scale = MLA_QK_DIM ** -0.5
    nblk = S // Q_BLOCK
    qb = jnp.moveaxis(q.reshape(B, nblk, Q_BLOCK, MLA_HEADS, MLA_QK_DIM), 1, 0)

    def blk(qi):
        s = jnp.einsum('bqhd,bkhd->bhqk', qi, k, preferred_element_type=jnp.float32) * scale
        p = jax.nn.softmax(s, axis=-1).astype(v.dtype)
        return jnp.einsum('bhqk,bkhd->bqhd', p, v)

    out = lax.map(blk, qb)
    return jnp.moveaxis(out, 0, 1).reshape(B, S, MLA_WIDTH)


def _encoder(x, c, w_ada, b_ada, g_attn, w_in, rpb, g_q, w_uq, g_kv, w_ukv,
             g_out_na, g_out_mla, w_o, g_ffn, w_gate, w_up, w_down, g_final):
    for l in range(DEPTH):
        mod = jax.nn.silu(c) @ w_ada[l] + b_ada[l]
        sh1, sc1, gt1, sh2, sc2, gt2 = [m[:, None, :] for m in jnp.split(mod, 6, axis=-1)]
        h = _rmsnorm(x, g_attn[l]) * (1 + sc1) + sh1
        proj = h @ w_in[l]
        B, S, _ = proj.shape
        o = 0
        q_na = proj[..., 0:NA_WIDTH]
        k_na = proj[..., NA_WIDTH:2 * NA_WIDTH]
        v_na = proj[..., 2 * NA_WIDTH:3 * NA_WIDTH]
        o = 3 * NA_WIDTH
        q_c = proj[..., o:o + Q_LORA_RANK]
        o += Q_LORA_RANK
        kv_c = proj[..., o:o + KV_LORA_RANK]
        o += KV_LORA_RANK
        k_pe = proj[..., o:o + QK_ROPE_DIM]
        hs = (B, S, NA_HEADS, HEAD_DIM)
        o_na = _neighbourhood_attention(q_na.reshape(hs), k_na.reshape(hs), v_na.reshape(hs), rpb[l])
        o_mla = _mla(q_c, kv_c, k_pe, g_q[l], w_uq[l], g_kv[l], w_ukv[l])
        merged = jnp.concatenate([_rmsnorm(o_na, g_out_na[l]), _rmsnorm(o_mla, g_out_mla[l])], axis=-1)
        x = x + gt1 * (merged @ w_o[l])
        h = _rmsnorm(x, g_ffn[l]) * (1 + sc2) + sh2
        f = (jax.nn.silu(h @ w_gate[l]) * (h @ w_up[l])) @ w_down[l]
        x = x + gt2 * f
    return _rmsnorm(x, g_final)


def setup_inputs(seed: int = 0) -> dict:
    key = jax.random.key(seed)
    ks = jax.random.split(key, 24)
    f32 = jnp.float32

    def nrm(k, shape, s):
        return jax.random.normal(k, shape, f32) * s

    def gain(k, shape):
        return 1.0 + 0.02 * jax.random.normal(k, shape, f32)

    L, D = DEPTH, D_MODEL
    return {
        "x_prompt": nrm(ks[0], (BATCH, SEQ, D), 1.0),
        "x_sample": nrm(ks[1], (DEC_BATCH, DEC_SEQ, D), 1.0),
        "c_prompt": nrm(ks[2], (BATCH, D), 1.0),
        "c_sample": nrm(ks[3], (DEC_BATCH, D), 1.0),
        "w_ada": nrm(ks[4], (L, D, 6 * D), 0.5 * D ** -0.5),
        "b_ada": nrm(ks[5], (L, 6 * D), 0.01),
        "g_attn": gain(ks[6], (L, D)),
        "w_in": nrm(ks[7], (L, D, IN_PROJ_WIDTH), D ** -0.5),
        "rpb": nrm(ks[8], (L, NA_HEADS, 2 * NA_KH_MAX - 1, 2 * NA_KW - 1), 0.1),
        "g_q": gain(ks[9], (L, Q_LORA_RANK)),
        "w_uq": nrm(ks[10], (L, Q_LORA_RANK, MLA_HEADS * MLA_QK_DIM), Q_LORA_RANK ** -0.5),
        "g_kv": gain(ks[11], (L, KV_LORA_RANK)),
        "w_ukv": nrm(ks[12], (L, KV_LORA_RANK, MLA_HEADS * (QK_NOPE_DIM + V_HEAD_DIM)), KV_LORA_RANK ** -0.5),
        "g_out_na": gain(ks[13], (L, NA_WIDTH)),
        "g_out_mla": gain(ks[14], (L, MLA_WIDTH)),
        "w_o": nrm(ks[15], (L, MIX_WIDTH, D), MIX_WIDTH ** -0.5),
        "g_ffn": gain(ks[16], (L, D)),
        "w_gate": nrm(ks[17], (L, D, D_FF), D ** -0.5),
        "w_up": nrm(ks[18], (L, D, D_FF), D ** -0.5),
        "w_down": nrm(ks[19], (L, D_FF, D), D_FF ** -0.5),
        "g_final": gain(ks[20], (D,)),
    }


def reference(x_prompt, x_sample, c_prompt, c_sample, w_ada, b_ada, g_attn, w_in, rpb,
              g_q, w_uq, g_kv, w_ukv, g_out_na, g_out_mla, w_o, g_ffn, w_gate, w_up,
              w_down, g_final):
    y_prompt = _encoder(x_prompt, c_prompt, w_ada, b_ada, g_attn, w_in, rpb, g_q, w_uq,
                        g_kv, w_ukv, g_out_na, g_out_mla, w_o, g_ffn, w_gate, w_up,
                        w_down, g_final)
    y_sample = _encoder(x_sample, c_sample, w_ada, b_ada, g_attn, w_in, rpb, g_q, w_uq,
                        g_kv, w_ukv, g_out_na, g_out_mla, w_o, g_ffn, w_gate, w_up,
                        w_down, g_final)
    return (y_prompt, y_sample)
```

```python
import functools

import jax
import jax.numpy as jnp
from jax import lax
from jax.experimental import pallas as pl
from jax.experimental.pallas import tpu as pltpu

D_MODEL = 2048
DEPTH = 1
GRID_W = 64
HEAD_DIM = 128
NA_HEADS = 8
NA_WIDTH = NA_HEADS * HEAD_DIM
NA_KH = 8
NA_KW = 16
MLA_HEADS = 8
Q_LORA_RANK = 512
KV_LORA_RANK = 512
QK_NOPE_DIM = 128
QK_ROPE_DIM = 64
V_HEAD_DIM = 128
MLA_QK_DIM = QK_NOPE_DIM + QK_ROPE_DIM
MLA_WIDTH = MLA_HEADS * V_HEAD_DIM
D_FF = 5632
ROPE_THETA = 10000.0
RMS_EPS = 1e-6

PROJ_WIDTH = 3 * NA_WIDTH + Q_LORA_RANK + KV_LORA_RANK + 2 * QK_ROPE_DIM

V7X_VMEM_BYTES = 64 * 1024 * 1024
V7X_LANES = 128
NEG = -1e30

BF16 = jnp.bfloat16
F32 = jnp.float32
_NT = (((1,), (1,)), ((), ()))


def _vmem_limit(*buffer_bytes):
    need = int(sum(buffer_bytes) * 1.5) + (4 << 20)
    return min(max(need, 32 << 20), V7X_VMEM_BYTES - (8 << 20))


def _params(semantics, *buffer_bytes):
    return pltpu.CompilerParams(dimension_semantics=semantics,
                                vmem_limit_bytes=_vmem_limit(*buffer_bytes))


def _rms(x, g):
    return x * lax.rsqrt(jnp.mean(x * x, axis=-1, keepdims=True) + RMS_EPS) * g


ADA_TN = 512


def _adaln_kernel(c_ref, w_ref, b_ref, o_ref):
    c = c_ref[...]
    cs = c * jax.nn.sigmoid(c)
    w = w_ref[...]
    r0 = jnp.sum(w * cs[:, 0:1], axis=0, keepdims=True)
    r1 = jnp.sum(w * cs[:, 1:2], axis=0, keepdims=True)
    o_ref[...] = jnp.concatenate([r0, r1], axis=0) + b_ref[...]


def _adaln(c_cols, w, b):
    d, n = w.shape
    return pl.pallas_call(
        _adaln_kernel,
        grid=(n // ADA_TN,),
        in_specs=[pl.BlockSpec((d, 2), lambda j: (0, 0)),
                  pl.BlockSpec((d, ADA_TN), lambda j: (0, j)),
                  pl.BlockSpec((1, ADA_TN), lambda j: (0, j))],
        out_specs=pl.BlockSpec((2, ADA_TN), lambda j: (0, j)),
        out_shape=jax.ShapeDtypeStruct((2, n), F32),
        compiler_params=_params(("arbitrary",), 2 * d * ADA_TN * 4, 2 * d * V7X_LANES * 4),
        name="adaln",
    )(c_cols, w, b)


INPROJ_TM = 1024
INPROJ_TN = PROJ_WIDTH // 3


def _inproj_kernel(x_ref, g_ref, sc_ref, sh_ref, w_ref, o_ref, hn_ref):
    @pl.when(pl.program_id(1) == 0)
    def _():
        h = _rms(x_ref[...], g_ref[...]) * (1.0 + sc_ref[...]) + sh_ref[...]
        hn_ref[...] = h.astype(BF16)

    o_ref[...] = jnp.dot(hn_ref[...], w_ref[...], preferred_element_type=F32).astype(o_ref.dtype)


def _inproj(x, g, sc, sh, w):
    s, d = x.shape
    tm, tn = min(INPROJ_TM, s), INPROJ_TN
    row = lambda i, j: (0, 0)
    return pl.pallas_call(
        _inproj_kernel,
        grid=(s // tm, PROJ_WIDTH // tn),
        in_specs=[pl.BlockSpec((tm, d), lambda i, j: (i, 0)),
                  pl.BlockSpec((1, d), row), pl.BlockSpec((1, d), row), pl.BlockSpec((1, d), row),
                  pl.BlockSpec((d, tn), lambda i, j: (0, j))],
        out_specs=pl.BlockSpec((tm, tn), lambda i, j: (i, j)),
        out_shape=jax.ShapeDtypeStruct((s, PROJ_WIDTH), BF16),
        scratch_shapes=[pltpu.VMEM((tm, d), BF16)],
        compiler_params=_params(("parallel", "arbitrary"),
                                2 * tm * d * 4, tm * d * 2, 2 * d * tn * 2, 2 * tm * tn * 2),
        name="inproj",
    )(x, g, sc, sh, w)


NA_G = 8
NA_KR = NA_G + NA_KH
NA_KBLK = 4
NA_NKB = NA_KR // NA_KBLK
NA_PT_LEFT_MASKED = 14
NA_PT_RIGHT_MASKED = 15
NA_PT_ENTRIES = 16
RPB_ROWS = 2 * NA_KH - 1
RPB_COLS = 2 * NA_KW - 1


def _na_bias_kernel(rpb_ref, o_ref):
    h = pl.program_id(0)
    qc = lax.broadcasted_iota(jnp.int32, (GRID_W, 2 * GRID_W), 0)
    kk = lax.broadcasted_iota(jnp.int32, (GRID_W, 2 * GRID_W), 1)
    right = kk >= GRID_W
    kc = jnp.where(right, kk - GRID_W, kk)
    dc = kc - qc + (NA_KW - 1)
    c0 = jnp.clip(qc - NA_KW // 2, 0, GRID_W - NA_KW)
    col_ok = (kc >= c0) & (kc < c0 + NA_KW)

    def tile(dr_left, dr_right):
        acc = jnp.full((GRID_W, 2 * GRID_W), NEG, F32)
        for d in range(RPB_COLS):
            base = h * (RPB_ROWS * RPB_COLS) + d
            left = NEG if dr_left is None else rpb_ref[base + dr_left * RPB_COLS]
            rght = NEG if dr_right is None else rpb_ref[base + dr_right * RPB_COLS]
            acc = jnp.where(dc == d, jnp.where(right, rght, left), acc)
        return jnp.where(col_ok, acc, NEG)

    for e in range(NA_PT_LEFT_MASKED):
        o_ref[0, e] = tile(e, e + 1)
    o_ref[0, NA_PT_LEFT_MASKED] = tile(None, 3)
    o_ref[0, NA_PT_RIGHT_MASKED] = tile(10, None)


def _na_bias(rpb):
    return pl.pallas_call(
        _na_bias_kernel,
        grid=(NA_HEADS,),
        in_specs=[pl.BlockSpec(memory_space=pltpu.SMEM)],
        out_specs=pl.BlockSpec((1, NA_PT_ENTRIES, GRID_W, 2 * GRID_W), lambda h: (h, 0, 0, 0)),
        out_shape=jax.ShapeDtypeStruct((NA_HEADS, NA_PT_ENTRIES, GRID_W, 2 * GRID_W), F32),
        compiler_params=_params(("arbitrary",)),
        name="na_bias",
    )(rpb.reshape(-1))


def _na_row_plan(q_off):
    plan = []
    for i in range(NA_G):
        qr = q_off + i
        rs = min(max(qr - NA_KH // 2, 0), NA_KR - NA_KH)
        p_lo, p_hi = rs // 2, (rs + NA_KH - 1) // 2
        entries = []
        for p in range(p_lo, p_hi + 1):
            ok_l = rs <= 2 * p < rs + NA_KH
            ok_r = rs <= 2 * p + 1 < rs + NA_KH
            dr0 = 2 * p - qr + (NA_KH - 1)
            if ok_l and ok_r:
                assert 0 <= dr0 < NA_PT_LEFT_MASKED
                entries.append(dr0)
            elif ok_r:
                assert dr0 + 1 == 3
                entries.append(NA_PT_LEFT_MASKED)
            else:
                assert ok_l and dr0 == 10
                entries.append(NA_PT_RIGHT_MASKED)
        plan.append((p_lo, tuple(entries)))
    return plan


def _na_kernel(q_ref, k0, k1, k2, k3, v0, v1, v2, v3, pt_ref, o_ref, s_ref, p_ref, l_ref):
    g, h = pl.program_id(0), pl.program_id(1)
    last = pl.num_programs(0) - 1
    scale = HEAD_DIM ** -0.5
    q = (q_ref[...].astype(F32) * scale).astype(BF16)
    k = jnp.concatenate([k0[...], k1[...], k2[...], k3[...]], axis=0)
    s_ref[...] = lax.dot_general(q, k, _NT, preferred_element_type=F32)
    p_ref[...] = jnp.zeros_like(p_ref)

    def softmax_rows(q_off):
        for i, (p_lo, entries) in enumerate(_na_row_plan(q_off)):
            rows = slice(i * GRID_W, (i + 1) * GRID_W)
            cols = slice(p_lo * 2 * GRID_W, (p_lo + len(entries)) * 2 * GRID_W)
            bias = jnp.concatenate([pt_ref[h, e] for e in entries], axis=1)
            s = s_ref[rows, cols] + bias
            p = jnp.exp(s - jnp.max(s, axis=-1, keepdims=True))
            l_ref[rows, :] = jnp.sum(p, axis=-1, keepdims=True)
            p_ref[rows, cols] = p.astype(BF16)

    pl.when(g == 0)(functools.partial(softmax_rows, 0))
    pl.when((g > 0) & (g < last))(functools.partial(softmax_rows, NA_KH // 2))
    pl.when(g == last)(functools.partial(softmax_rows, NA_KH))

    v = jnp.concatenate([v0[...], v1[...], v2[...], v3[...]], axis=0)
    o = jnp.dot(p_ref[...], v, preferred_element_type=F32)
    o_ref[...] = o / l_ref[...]


def _na_attention(proj, pt):
    s = proj.shape[0]
    rows = s // GRID_W
    assert rows % NA_G == 0 and rows >= 2 * NA_G
    tq, tkb = NA_G * GRID_W, NA_KBLK * GRID_W
    n_kb = rows // NA_KBLK
    k_col, v_col = NA_HEADS, 2 * NA_HEADS

    def kv_spec(j, col0):
        def index(g, h):
            first = jnp.clip(g * (NA_G // NA_KBLK) - 1, 0, n_kb - NA_NKB)
            return (first + j, col0 + h)
        return pl.BlockSpec((tkb, HEAD_DIM), index)

    return pl.pallas_call(
        _na_kernel,
        grid=(rows // NA_G, NA_HEADS),
        in_specs=([pl.BlockSpec((tq, HEAD_DIM), lambda g, h: (g, h))]
                  + [kv_spec(j, k_col) for j in range(NA_NKB)]
                  + [kv_spec(j, v_col) for j in range(NA_NKB)]
                  + [pl.BlockSpec(pt.shape, lambda g, h: (0, 0, 0, 0))]),
        out_specs=pl.BlockSpec((tq, HEAD_DIM), lambda g, h: (g, h)),
        out_shape=jax.ShapeDtypeStruct((s, NA_WIDTH), F32),
        scratch_shapes=[pltpu.VMEM((tq, NA_KR * GRID_W), F32),
                        pltpu.VMEM((tq, NA_KR * GRID_W), BF16),
                        pltpu.VMEM((tq, 1), F32)],
        compiler_params=_params(("parallel", "arbitrary"),
                                2 * pt.size * 4, tq * NA_KR * GRID_W * 6, tq * V7X_LANES * 4,
                                2 * 3 * tq * HEAD_DIM * 4),
        name="na_attention",
    )(proj, *([proj] * (2 * NA_NKB)), pt)


MLAPREP_TM = 512


def _mla_prep_kernel(qc_ref, kvc_ref, kpe_ref, gq_ref, gkv_ref, wqn_ref, wqp_ref, wqr_ref,
                     wkn_ref, wv_ref, cc_ref, ss_ref, cs_ref, q_ref, k_ref, v_ref):
    scale = MLA_QK_DIM ** -0.5
    qn = _rms(qc_ref[...].astype(F32), gq_ref[...]).astype(BF16)
    kvn = _rms(kvc_ref[...].astype(F32), gkv_ref[...]).astype(BF16)

    prod = kpe_ref[...].astype(F32) * cs_ref[...]
    k_pe = (prod + pltpu.roll(prod, QK_ROPE_DIM, 1))[:, :QK_ROPE_DIM].astype(BF16)

    cos = jnp.concatenate([cc_ref[...]] * (MLA_HEADS // 2), axis=1)
    sin = jnp.concatenate([ss_ref[...]] * (MLA_HEADS // 2), axis=1)
    q_pe = (jnp.dot(qn, wqp_ref[...], preferred_element_type=F32) * cos
            + jnp.dot(qn, wqr_ref[...], preferred_element_type=F32) * sin)
    q_pe = (q_pe * scale).astype(BF16)

    q_nope = (jnp.dot(qn, wqn_ref[...], preferred_element_type=F32) * scale).astype(BF16)
    k_nope = jnp.dot(kvn, wkn_ref[...], preferred_element_type=F32).astype(BF16)
    v = jnp.dot(kvn, wv_ref[...], preferred_element_type=F32).astype(BF16)
    for h in range(MLA_HEADS):
        nope = slice(h * QK_NOPE_DIM, (h + 1) * QK_NOPE_DIM)
        q_ref[h, :, :QK_NOPE_DIM] = q_nope[:, nope]
        q_ref[h, :, QK_NOPE_DIM:] = q_pe[:, h * QK_ROPE_DIM:(h + 1) * QK_ROPE_DIM]
        k_ref[h, :, :QK_NOPE_DIM] = k_nope[:, nope]
        k_ref[h, :, QK_NOPE_DIM:] = k_pe
        v_ref[h] = v[:, h * V_HEAD_DIM:(h + 1) * V_HEAD_DIM]


def _mla_prep(proj, g_q, g_kv, wqn, wqp, wqr, wkn, wv, cc, ss, cs):
    s = proj.shape[0]
    tm = min(MLAPREP_TM, s)
    qc_blk = 3 * NA_WIDTH // Q_LORA_RANK
    kpe_blk = (3 * NA_WIDTH + Q_LORA_RANK + KV_LORA_RANK) // (2 * QK_ROPE_DIM)
    const = lambda i: (0, 0)
    return pl.pallas_call(
        _mla_prep_kernel,
        grid=(s // tm,),
        in_specs=[pl.BlockSpec((tm, Q_LORA_RANK), lambda i: (i, qc_blk)),
                  pl.BlockSpec((tm, KV_LORA_RANK), lambda i: (i, qc_blk + 1)),
                  pl.BlockSpec((tm, 2 * QK_ROPE_DIM), lambda i: (i, kpe_blk)),
                  pl.BlockSpec((1, Q_LORA_RANK), const), pl.BlockSpec((1, KV_LORA_RANK), const),
                  pl.BlockSpec(wqn.shape, const), pl.BlockSpec(wqp.shape, const),
                  pl.BlockSpec(wqr.shape, const), pl.BlockSpec(wkn.shape, const),
                  pl.BlockSpec(wv.shape, const),
                  pl.BlockSpec((tm, 2 * QK_ROPE_DIM), lambda i: (i, 0)),
                  pl.BlockSpec((tm, 2 * QK_ROPE_DIM), lambda i: (i, 0)),
                  pl.BlockSpec((tm, 2 * QK_ROPE_DIM), lambda i: (i, 0))],
        out_specs=[pl.BlockSpec((MLA_HEADS, tm, MLA_QK_DIM), lambda i: (0, i, 0)),
                   pl.BlockSpec((MLA_HEADS, tm, MLA_QK_DIM), lambda i: (0, i, 0)),
                   pl.BlockSpec((MLA_HEADS, tm, V_HEAD_DIM), lambda i: (0, i, 0))],
        out_shape=[jax.ShapeDtypeStruct((MLA_HEADS, s, MLA_QK_DIM), BF16),
                   jax.ShapeDtypeStruct((MLA_HEADS, s, MLA_QK_DIM), BF16),
                   jax.ShapeDtypeStruct((MLA_HEADS, s, V_HEAD_DIM), BF16)],
        compiler_params=_params(("parallel",), 2 * 2 * 512 * 4096, 2 * 3 * MLA_HEADS * tm * 256 * 2,
                                2 * tm * 1536 * 4, 8 * tm * 1024 * 4),
        name="mla_prep",
    )(proj, proj, proj, g_q, g_kv, wqn, wqp, wqr, wkn, wv, cc, ss, cs)


FLASH_TQ = 512
FLASH_TK = 512


def _flash_kernel(q_ref, k_ref, v_ref, o_ref, *, tk):
    q = q_ref[0]
    tq = q.shape[0]
    n_k = k_ref.shape[1] // tk

    def body(j, carry):
        m, l, acc = carry
        start = pl.multiple_of(j * tk, tk)
        k = k_ref[0, pl.ds(start, tk), :]
        v = v_ref[0, pl.ds(start, tk), :]
        s = lax.dot_general(q, k, _NT, preferred_element_type=F32)
        m_new = jnp.maximum(m, jnp.max(s, axis=-1, keepdims=True))
        alpha = jnp.exp(m - m_new)
        p = jnp.exp(s - m_new)
        l = alpha * l + jnp.sum(p, axis=-1, keepdims=True)
        acc = alpha * acc + jnp.dot(p.astype(BF16), v, preferred_element_type=F32)
        return m_new, l, acc

    m0 = jnp.full((tq, 1), -jnp.inf, F32)
    l0 = jnp.zeros((tq, 1), F32)
    acc0 = jnp.zeros((tq, V_HEAD_DIM), F32)
    _, l, acc = lax.fori_loop(0, n_k, body, (m0, l0, acc0))
    o_ref[...] = acc / l


def _flash(q, k, v):
    _, s, _ = q.shape
    tq, tk = min(FLASH_TQ, s), min(FLASH_TK, s)
    return pl.pallas_call(
        functools.partial(_flash_kernel, tk=tk),
        grid=(MLA_HEADS, s // tq),
        in_specs=[pl.BlockSpec((1, tq, MLA_QK_DIM), lambda h, i: (h, i, 0)),
                  pl.BlockSpec((1, s, MLA_QK_DIM), lambda h, i: (h, 0, 0)),
                  pl.BlockSpec((1, s, V_HEAD_DIM), lambda h, i: (h, 0, 0))],
        out_specs=pl.BlockSpec((tq, V_HEAD_DIM), lambda h, i: (i, h)),
        out_shape=jax.ShapeDtypeStruct((s, MLA_WIDTH), F32),
        compiler_params=_params(("parallel", "arbitrary"),
                                2 * s * 256 * 2, 2 * s * V_HEAD_DIM * 2, 4 * tq * tk * 4),
        name="mla_flash",
    )(q, k, v)


OUTPROJ_TM = 256


def _outproj_kernel(na_ref, mla_ref, x_ref, gna_ref, gmla_ref, gt_ref, w_ref, o_ref):
    na = _rms(na_ref[...], gna_ref[...]).astype(BF16)
    mla = _rms(mla_ref[...], gmla_ref[...]).astype(BF16)
    y = (jnp.dot(na, w_ref[:NA_WIDTH, :], preferred_element_type=F32)
         + jnp.dot(mla, w_ref[NA_WIDTH:, :], preferred_element_type=F32))
    o_ref[...] = x_ref[...] + gt_ref[...] * y


def _outproj(o_na, o_mla, x, g_na, g_mla, gt, w):
    s, d = x.shape
    tm = min(OUTPROJ_TM, s)
    const = lambda i: (0, 0)
    return pl.pallas_call(
        _outproj_kernel,
        grid=(s // tm,),
        in_specs=[pl.BlockSpec((tm, NA_WIDTH), lambda i: (i, 0)),
                  pl.BlockSpec((tm, MLA_WIDTH), lambda i: (i, 0)),
                  pl.BlockSpec((tm, d), lambda i: (i, 0)),
                  pl.BlockSpec((1, NA_WIDTH), const), pl.BlockSpec((1, MLA_WIDTH), const),
                  pl.BlockSpec((1, d), const), pl.BlockSpec(w.shape, const)],
        out_specs=pl.BlockSpec((tm, d), lambda i: (i, 0)),
        out_shape=jax.ShapeDtypeStruct((s, d), F32),
        compiler_params=_params(("parallel",), 2 * w.size * 2, 2 * 3 * tm * d * 4),
        name="outproj",
    )(o_na, o_mla, x, g_na, g_mla, gt, w)


FFN_TM = 512
FFN_TF = 512


def _ffn_kernel(x_ref, g_ref, sc_ref, sh_ref, gt_ref, gfin_ref, wg_ref, wu_ref, wd_ref, o_ref,
                hn_ref, acc_ref):
    j = pl.program_id(1)

    @pl.when(j == 0)
    def _():
        h = _rms(x_ref[...], g_ref[...]) * (1.0 + sc_ref[...]) + sh_ref[...]
        hn_ref[...] = h.astype(BF16)
        acc_ref[...] = jnp.zeros_like(acc_ref)

    h = hn_ref[...]
    a = jnp.dot(h, wg_ref[...], preferred_element_type=F32)
    b = jnp.dot(h, wu_ref[...], preferred_element_type=F32)
    t = (a * jax.nn.sigmoid(a) * b).astype(BF16)
    acc_ref[...] += jnp.dot(t, wd_ref[...], preferred_element_type=F32)

    @pl.when(j == pl.num_programs(1) - 1)
    def _():
        o_ref[...] = _rms(x_ref[...] + gt_ref[...] * acc_ref[...], gfin_ref[...])


def _ffn(x, g, sc, sh, gt, g_final, wg, wu, wd):
    s, d = x.shape
    tm, tf = min(FFN_TM, s), FFN_TF
    const = lambda i, j: (0, 0)
    vec = pl.BlockSpec((1, d), const)
    return pl.pallas_call(
        _ffn_kernel,
        grid=(s // tm, D_FF // tf),
        in_specs=[pl.BlockSpec((tm, d), lambda i, j: (i, 0)), vec, vec, vec, vec, vec,
                  pl.BlockSpec((d, tf), lambda i, j: (0, j)),
                  pl.BlockSpec((d, tf), lambda i, j: (0, j)),
                  pl.BlockSpec((tf, d), lambda i, j: (j, 0))],
        out_specs=pl.BlockSpec((tm, d), lambda i, j: (i, 0)),
        out_shape=jax.ShapeDtypeStruct((s, d), F32),
        scratch_shapes=[pltpu.VMEM((tm, d), BF16), pltpu.VMEM((tm, d), F32)],
        compiler_params=_params(("parallel", "arbitrary"),
                                4 * tm * d * 4, tm * d * 6, 2 * 3 * d * tf * 2, 3 * tm * tf * 4),
        name="ffn",
    )(x, g, sc, sh, gt, g_final, wg, wu, wd)


def _rotate_half_cols(w):
    half = QK_ROPE_DIM // 2
    return jnp.concatenate([-w[..., half:], w[..., :half]], axis=-1)


def _rope_tables(s):
    pos = jnp.arange(s, dtype=F32)
    inv_freq = ROPE_THETA ** (-jnp.arange(0, QK_ROPE_DIM, 2, dtype=F32) / QK_ROPE_DIM)
    ang = pos[:, None] * inv_freq[None, :]
    ang = jnp.concatenate([ang, ang], axis=-1)
    cos, sin = jnp.cos(ang), jnp.sin(ang)
    return (jnp.concatenate([cos, cos], axis=-1), jnp.concatenate([sin, sin], axis=-1),
            jnp.concatenate([cos, sin], axis=-1))


def _layer_weights(l, w_in, w_uq, w_ukv, w_o, w_gate, w_up, w_down):
    k_pe0 = 3 * NA_WIDTH + Q_LORA_RANK + KV_LORA_RANK
    w_in_l = w_in[l]
    w_in_ext = jnp.concatenate([w_in_l, _rotate_half_cols(w_in_l[:, k_pe0:])], axis=1).astype(BF16)
    uq = w_uq[l].reshape(Q_LORA_RANK, MLA_HEADS, MLA_QK_DIM)
    wqn = uq[:, :, :QK_NOPE_DIM].reshape(Q_LORA_RANK, -1).astype(BF16)
    wqp = uq[:, :, QK_NOPE_DIM:]
    wqr = _rotate_half_cols(wqp).reshape(Q_LORA_RANK, -1).astype(BF16)
    wqp = wqp.reshape(Q_LORA_RANK, -1).astype(BF16)
    ukv = w_ukv[l].reshape(KV_LORA_RANK, MLA_HEADS, QK_NOPE_DIM + V_HEAD_DIM)
    wkn = ukv[:, :, :QK_NOPE_DIM].reshape(KV_LORA_RANK, -1).astype(BF16)
    wv = ukv[:, :, QK_NOPE_DIM:].reshape(KV_LORA_RANK, -1).astype(BF16)
    return dict(w_in=w_in_ext, wqn=wqn, wqp=wqp, wqr=wqr, wkn=wkn, wv=wv,
                w_o=w_o[l].astype(BF16), w_gate=w_gate[l].astype(BF16),
                w_up=w_up[l].astype(BF16), w_down=w_down[l].astype(BF16))


def _encoder_layer(x, mod, lw, pt, tables, g_attn, g_q, g_kv, g_out_na, g_out_mla, g_ffn, g_final):
    d = D_MODEL
    sh1, sc1, gt1, sh2, sc2, gt2 = [mod[:, i * d:(i + 1) * d] for i in range(6)]
    proj = _inproj(x, g_attn, sc1, sh1, lw["w_in"])
    o_na = _na_attention(proj, pt)
    q, k, v = _mla_prep(proj, g_q, g_kv, lw["wqn"], lw["wqp"], lw["wqr"], lw["wkn"], lw["wv"], *tables)
    o_mla = _flash(q, k, v)
    x = _outproj(o_na, o_mla, x, g_out_na, g_out_mla, gt1, lw["w_o"])
    return _ffn(x, g_ffn, sc2, sh2, gt2, g_final, lw["w_gate"], lw["w_up"], lw["w_down"])


def kernel(x_prompt, x_sample, c_prompt, c_sample, w_ada, b_ada, g_attn, w_in, rpb, g_q, w_uq,
           g_kv, w_ukv, g_out_na, g_out_mla, w_o, g_ffn, w_gate, w_up, w_down, g_final):
    assert DEPTH == 1 and w_ada.shape[0] == DEPTH
    assert x_prompt.shape[0] == 1 and x_sample.shape[0] == 1
    l = 0
    c_cols = jnp.concatenate([c_prompt, c_sample], axis=0).T
    mod = _adaln(c_cols, w_ada[l], b_ada[l][None, :])
    lw = _layer_weights(l, w_in, w_uq, w_ukv, w_o, w_gate, w_up, w_down)
    pt = _na_bias(rpb[l])
    row = lambda a: a[None, :]
    outs = []
    for b, x in enumerate((x_prompt, x_sample)):
        y = _encoder_layer(x[0], mod[b:b + 1], lw, pt, _rope_tables(x.shape[1]),
                           row(g_attn[l]), row(g_q[l]), row(g_kv[l]), row(g_out_na[l]),
                           row(g_out_mla[l]), row(g_ffn[l]), row(g_final))
        outs.append(y[None])
    return tuple(outs)
```

```python
import functools

import jax
import jax.numpy as jnp
from jax import lax
from jax.experimental import pallas as pl
from jax.experimental.pallas import tpu as pltpu

D_MODEL = 2048
DEPTH = 1
GRID_W = 64
HEAD_DIM = 128
NA_HEADS = 8
NA_WIDTH = NA_HEADS * HEAD_DIM
NA_KH = 8
NA_KW = 16
MLA_HEADS = 8
Q_LORA_RANK = 512
KV_LORA_RANK = 512
QK_NOPE_DIM = 128
QK_ROPE_DIM = 64
V_HEAD_DIM = 128
MLA_QK_DIM = QK_NOPE_DIM + QK_ROPE_DIM
MLA_WIDTH = MLA_HEADS * V_HEAD_DIM
D_FF = 5632
ROPE_THETA = 10000.0
RMS_EPS = 1e-6

PROJ_WIDTH = 3 * NA_WIDTH + Q_LORA_RANK + KV_LORA_RANK + 2 * QK_ROPE_DIM

V7X_VMEM_BYTES = 64 * 1024 * 1024
V7X_LANES = 128
NEG = -1e30
LOG2_E = 1.4426950408889634

BF16 = jnp.bfloat16
F32 = jnp.float32
_NT = (((1,), (1,)), ((), ()))


def _vmem_limit(*buffer_bytes):
    need = int(sum(buffer_bytes) * 1.5) + (4 << 20)
    return min(max(need, 32 << 20), V7X_VMEM_BYTES - (8 << 20))


def _params(semantics, *buffer_bytes):
    return pltpu.CompilerParams(dimension_semantics=semantics,
                                vmem_limit_bytes=_vmem_limit(*buffer_bytes))


def _rms(x, g):
    return x * lax.rsqrt(jnp.mean(x * x, axis=-1, keepdims=True) + RMS_EPS) * g


ADA_TN = 512


def _adaln_kernel(c_ref, w_ref, b_ref, o_ref):
    c = c_ref[...]
    cs = c * jax.nn.sigmoid(c)
    w = w_ref[...]
    r0 = jnp.sum(w * cs[:, 0:1], axis=0, keepdims=True)
    r1 = jnp.sum(w * cs[:, 1:2], axis=0, keepdims=True)
    o_ref[...] = jnp.concatenate([r0, r1], axis=0) + b_ref[...]


def _adaln(c_cols, w, b):
    d, n = w.shape
    return pl.pallas_call(
        _adaln_kernel,
        grid=(n // ADA_TN,),
        in_specs=[pl.BlockSpec((d, 2), lambda j: (0, 0)),
                  pl.BlockSpec((d, ADA_TN), lambda j: (0, j)),
                  pl.BlockSpec((1, ADA_TN), lambda j: (0, j))],
        out_specs=pl.BlockSpec((2, ADA_TN), lambda j: (0, j)),
        out_shape=jax.ShapeDtypeStruct((2, n), F32),
        compiler_params=_params(("arbitrary",), 2 * d * ADA_TN * 4, 2 * d * V7X_LANES * 4),
        name="adaln",
    )(c_cols, w, b)


INPROJ_TM = 1024
INPROJ_TN = PROJ_WIDTH // 3


def _inproj_kernel(x_ref, g_ref, sc_ref, sh_ref, w_ref, o_ref, hn_ref):
    @pl.when(pl.program_id(1) == 0)
    def _():
        h = _rms(x_ref[...], g_ref[...]) * (1.0 + sc_ref[...]) + sh_ref[...]
        hn_ref[...] = h.astype(BF16)

    o_ref[...] = jnp.dot(hn_ref[...], w_ref[...], preferred_element_type=F32).astype(o_ref.dtype)


def _inproj(x, g, sc, sh, w):
    s, d = x.shape
    tm, tn = min(INPROJ_TM, s), INPROJ_TN
    row = lambda i, j: (0, 0)
    return pl.pallas_call(
        _inproj_kernel,
        grid=(s // tm, PROJ_WIDTH // tn),
        in_specs=[pl.BlockSpec((tm, d), lambda i, j: (i, 0)),
                  pl.BlockSpec((1, d), row), pl.BlockSpec((1, d), row), pl.BlockSpec((1, d), row),
                  pl.BlockSpec((d, tn), lambda i, j: (0, j))],
        out_specs=pl.BlockSpec((tm, tn), lambda i, j: (i, j)),
        out_shape=jax.ShapeDtypeStruct((s, PROJ_WIDTH), BF16),
        scratch_shapes=[pltpu.VMEM((tm, d), BF16)],
        compiler_params=_params(("parallel", "arbitrary"),
                                2 * tm * d * 4, tm * d * 2, 2 * d * tn * 2, 2 * tm * tn * 2),
        name="inproj",
    )(x, g, sc, sh, w)


NA_G = 8
NA_KR = NA_G + NA_KH
NA_KBLK = 4
NA_NKB = NA_KR // NA_KBLK
NA_PT_LEFT_MASKED = 14
NA_PT_RIGHT_MASKED = 15
NA_PT_ENTRIES = 16
RPB_ROWS = 2 * NA_KH - 1
RPB_COLS = 2 * NA_KW - 1


def _na_bias_kernel(rpb_ref, o_ref):
    h = pl.program_id(0)
    qc = lax.broadcasted_iota(jnp.int32, (GRID_W, 2 * GRID_W), 0)
    kk = lax.broadcasted_iota(jnp.int32, (GRID_W, 2 * GRID_W), 1)
    right = kk >= GRID_W
    kc = jnp.where(right, kk - GRID_W, kk)
    dc = kc - qc + (NA_KW - 1)
    c0 = jnp.clip(qc - NA_KW // 2, 0, GRID_W - NA_KW)
    col_ok = (kc >= c0) & (kc < c0 + NA_KW)

    def tile(dr_left, dr_right):
        acc = jnp.full((GRID_W, 2 * GRID_W), NEG, F32)
        for d in range(RPB_COLS):
            base = h * (RPB_ROWS * RPB_COLS) + d
            left = NEG if dr_left is None else rpb_ref[base + dr_left * RPB_COLS]
            rght = NEG if dr_right is None else rpb_ref[base + dr_right * RPB_COLS]
            acc = jnp.where(dc == d, jnp.where(right, rght, left), acc)
        return jnp.where(col_ok, acc, NEG)

    for e in range(NA_PT_LEFT_MASKED):
        o_ref[0, e] = tile(e, e + 1)
    o_ref[0, NA_PT_LEFT_MASKED] = tile(None, 3)
    o_ref[0, NA_PT_RIGHT_MASKED] = tile(10, None)


def _na_bias(rpb):
    return pl.pallas_call(
        _na_bias_kernel,
        grid=(NA_HEADS,),
        in_specs=[pl.BlockSpec(memory_space=pltpu.SMEM)],
        out_specs=pl.BlockSpec((1, NA_PT_ENTRIES, GRID_W, 2 * GRID_W), lambda h: (h, 0, 0, 0)),
        out_shape=jax.ShapeDtypeStruct((NA_HEADS, NA_PT_ENTRIES, GRID_W, 2 * GRID_W), F32),
        compiler_params=_params(("arbitrary",)),
        name="na_bias",
    )(rpb.reshape(-1))


def _na_row_plan(q_off):
    plan = []
    for i in range(NA_G):
        qr = q_off + i
        rs = min(max(qr - NA_KH // 2, 0), NA_KR - NA_KH)
        p_lo, p_hi = rs // 2, (rs + NA_KH - 1) // 2
        entries = []
        for p in range(p_lo, p_hi + 1):
            ok_l = rs <= 2 * p < rs + NA_KH
            ok_r = rs <= 2 * p + 1 < rs + NA_KH
            dr0 = 2 * p - qr + (NA_KH - 1)
            if ok_l and ok_r:
                assert 0 <= dr0 < NA_PT_LEFT_MASKED
                entries.append(dr0)
            elif ok_r:
                assert dr0 + 1 == 3
                entries.append(NA_PT_LEFT_MASKED)
            else:
                assert ok_l and dr0 == 10
                entries.append(NA_PT_RIGHT_MASKED)
        plan.append((p_lo, tuple(entries)))
    return plan


def _na_kernel(q_ref, k0, k1, k2, k3, v0, v1, v2, v3, pt_ref, o_ref, s_ref, p_ref, l_ref):
    g, h = pl.program_id(0), pl.program_id(1)
    last = pl.num_programs(0) - 1
    scale = HEAD_DIM ** -0.5
    q = (q_ref[...].astype(F32) * scale).astype(BF16)
    k = jnp.concatenate([k0[...], k1[...], k2[...], k3[...]], axis=0)
    s_ref[...] = lax.dot_general(q, k, _NT, preferred_element_type=F32)
    p_ref[...] = jnp.zeros_like(p_ref)

    def softmax_rows(q_off):
        for i, (p_lo, entries) in enumerate(_na_row_plan(q_off)):
            rows = slice(i * GRID_W, (i + 1) * GRID_W)
            cols = slice(p_lo * 2 * GRID_W, (p_lo + len(entries)) * 2 * GRID_W)
            bias = jnp.concatenate([pt_ref[h, e] for e in entries], axis=1)
            s = s_ref[rows, cols] + bias
            p = jnp.exp(s - jnp.max(s, axis=-1, keepdims=True))
            l_ref[rows, :] = jnp.sum(p, axis=-1, keepdims=True)
            p_ref[rows, cols] = p.astype(BF16)

    pl.when(g == 0)(functools.partial(softmax_rows, 0))
    pl.when((g > 0) & (g < last))(functools.partial(softmax_rows, NA_KH // 2))
    pl.when(g == last)(functools.partial(softmax_rows, NA_KH))

    v = jnp.concatenate([v0[...], v1[...], v2[...], v3[...]], axis=0)
    o = jnp.dot(p_ref[...], v, preferred_element_type=F32)
    o_ref[...] = o / l_ref[...]


def _na_attention(proj, pt):
    s = proj.shape[0]
    rows = s // GRID_W
    assert rows % NA_G == 0 and rows >= 2 * NA_G
    tq, tkb = NA_G * GRID_W, NA_KBLK * GRID_W
    n_kb = rows // NA_KBLK
    k_col, v_col = NA_HEADS, 2 * NA_HEADS

    def kv_spec(j, col0):
        def index(g, h):
            first = jnp.clip(g * (NA_G // NA_KBLK) - 1, 0, n_kb - NA_NKB)
            return (first + j, col0 + h)
        return pl.BlockSpec((tkb, HEAD_DIM), index)

    return pl.pallas_call(
        _na_kernel,
        grid=(rows // NA_G, NA_HEADS),
        in_specs=([pl.BlockSpec((tq, HEAD_DIM), lambda g, h: (g, h))]
                  + [kv_spec(j, k_col) for j in range(NA_NKB)]
                  + [kv_spec(j, v_col) for j in range(NA_NKB)]
                  + [pl.BlockSpec(pt.shape, lambda g, h: (0, 0, 0, 0))]),
        out_specs=pl.BlockSpec((tq, HEAD_DIM), lambda g, h: (g, h)),
        out_shape=jax.ShapeDtypeStruct((s, NA_WIDTH), F32),
        scratch_shapes=[pltpu.VMEM((tq, NA_KR * GRID_W), F32),
                        pltpu.VMEM((tq, NA_KR * GRID_W), BF16),
                        pltpu.VMEM((tq, 1), F32)],
        compiler_params=_params(("parallel", "arbitrary"),
                                2 * pt.size * 4, tq * NA_KR * GRID_W * 6, tq * V7X_LANES * 4,
                                2 * 3 * tq * HEAD_DIM * 4),
        name="na_attention",
    )(proj, *([proj] * (2 * NA_NKB)), pt)


MLAPREP_TM = 512
V_ONES_ROWS = 16


def _mla_prep_kernel(qc_ref, kvc_ref, kpe_ref, gq_ref, gkv_ref, wqn_ref, wqp_ref, wqr_ref,
                     wkn_ref, wv_ref, cc_ref, ss_ref, cs_ref, q_ref, k_ref, v_ref):
    scale = MLA_QK_DIM ** -0.5 * LOG2_E
    qn = _rms(qc_ref[...].astype(F32), gq_ref[...]).astype(BF16)
    kvn = _rms(kvc_ref[...].astype(F32), gkv_ref[...]).astype(BF16)

    prod = kpe_ref[...].astype(F32) * cs_ref[...]
    k_pe = (prod + pltpu.roll(prod, QK_ROPE_DIM, 1))[:, :QK_ROPE_DIM].astype(BF16)

    cos = jnp.concatenate([cc_ref[...]] * (MLA_HEADS // 2), axis=1)
    sin = jnp.concatenate([ss_ref[...]] * (MLA_HEADS // 2), axis=1)
    q_pe = (jnp.dot(qn, wqp_ref[...], preferred_element_type=F32) * cos
            + jnp.dot(qn, wqr_ref[...], preferred_element_type=F32) * sin)
    q_pe_t = (q_pe * scale).T.astype(BF16)

    q_nope = jnp.dot(qn, wqn_ref[...], preferred_element_type=F32) * scale
    q_nope_t = q_nope.T.astype(BF16)
    k_nope = jnp.dot(kvn, wkn_ref[...], preferred_element_type=F32).astype(BF16)
    v_t = jnp.dot(kvn, wv_ref[...], preferred_element_type=F32).T.astype(BF16)
    for h in range(MLA_HEADS):
        nope = slice(h * QK_NOPE_DIM, (h + 1) * QK_NOPE_DIM)
        q_ref[h, :QK_NOPE_DIM, :] = q_nope_t[nope, :]
        q_ref[h, QK_NOPE_DIM:, :] = q_pe_t[h * QK_ROPE_DIM:(h + 1) * QK_ROPE_DIM, :]
        k_ref[h, :, :QK_NOPE_DIM] = k_nope[:, nope]
        k_ref[h, :, QK_NOPE_DIM:] = k_pe
        v_ref[h, 0, :V_HEAD_DIM, :] = v_t[h * V_HEAD_DIM:(h + 1) * V_HEAD_DIM, :]
        v_ref[h, 0, V_HEAD_DIM:, :] = jnp.ones((V_ONES_ROWS, v_t.shape[1]), BF16)


def _mla_prep(proj, g_q, g_kv, wqn, wqp, wqr, wkn, wv, cc, ss, cs):
    s = proj.shape[0]
    tm = min(MLAPREP_TM, s)
    qc_blk = 3 * NA_WIDTH // Q_LORA_RANK
    kpe_blk = (3 * NA_WIDTH + Q_LORA_RANK + KV_LORA_RANK) // (2 * QK_ROPE_DIM)
    const = lambda i: (0, 0)
    return pl.pallas_call(
        _mla_prep_kernel,
        grid=(s // tm,),
        in_specs=[pl.BlockSpec((tm, Q_LORA_RANK), lambda i: (i, qc_blk)),
                  pl.BlockSpec((tm, KV_LORA_RANK), lambda i: (i, qc_blk + 1)),
                  pl.BlockSpec((tm, 2 * QK_ROPE_DIM), lambda i: (i, kpe_blk)),
                  pl.BlockSpec((1, Q_LORA_RANK), const), pl.BlockSpec((1, KV_LORA_RANK), const),
                  pl.BlockSpec(wqn.shape, const), pl.BlockSpec(wqp.shape, const),
                  pl.BlockSpec(wqr.shape, const), pl.BlockSpec(wkn.shape, const),
                  pl.BlockSpec(wv.shape, const),
                  pl.BlockSpec((tm, 2 * QK_ROPE_DIM), lambda i: (i, 0)),
                  pl.BlockSpec((tm, 2 * QK_ROPE_DIM), lambda i: (i, 0)),
                  pl.BlockSpec((tm, 2 * QK_ROPE_DIM), lambda i: (i, 0))],
        out_specs=[pl.BlockSpec((MLA_HEADS, MLA_QK_DIM, tm), lambda i: (0, 0, i)),
                   pl.BlockSpec((MLA_HEADS, tm, MLA_QK_DIM), lambda i: (0, i, 0)),
                   pl.BlockSpec((MLA_HEADS, 1, V_HEAD_DIM + V_ONES_ROWS, tm), lambda i: (0, i, 0, 0))],
        out_shape=[jax.ShapeDtypeStruct((MLA_HEADS, MLA_QK_DIM, s), BF16),
                   jax.ShapeDtypeStruct((MLA_HEADS, s, MLA_QK_DIM), BF16),
                   jax.ShapeDtypeStruct((MLA_HEADS, s // tm, V_HEAD_DIM + V_ONES_ROWS, tm), BF16)],
        compiler_params=_params(("parallel",), 2 * 2 * 512 * 4096, 2 * 3 * MLA_HEADS * tm * 256 * 2,
                                2 * tm * 1536 * 4, 8 * tm * 1024 * 4),
        name="mla_prep",
    )(proj, proj, proj, g_q, g_kv, wqn, wqp, wqr, wkn, wv, cc, ss, cs)


FLASH_TQ = 512
FLASH_SLOTS = 8


def _flash_kernel(qt_ref, k_ref, vt_ref, o_ref, acc_ref, s_ref, p_ref):
    tq = qt_ref.shape[2]
    n_k, _, tk = vt_ref.shape[1:]
    n_slot = s_ref.shape[0]
    assert n_k % n_slot == 0

    def scores(j):
        k = k_ref[0, pl.ds(pl.multiple_of(j * tk, tk), tk), :]
        return jnp.dot(k, qt_ref[0], preferred_element_type=F32)

    def accumulate(j, slot, alpha):
        pv = jnp.dot(vt_ref[0, j], p_ref[slot], preferred_element_type=F32)
        acc_ref[...] = alpha * acc_ref[...] + pv

    def step(j, slot, carry):
        m, alpha_prev, tile_max = carry
        s_next = scores(jnp.minimum(j + 1, n_k - 1))
        s_ref[(slot + 1) % n_slot] = s_next
        tile_max_next = jnp.max(s_next, axis=0, keepdims=True)
        accumulate(jnp.maximum(j - 1, 0), (slot - 1) % n_slot, alpha_prev)
        m_new = jnp.maximum(m, tile_max)
        alpha = jnp.exp2(m - m_new)
        p_ref[slot] = jnp.exp2((s_ref[slot] - m_new).astype(BF16))
        return m_new, alpha, tile_max_next

    def body(i, carry):
        for slot in range(n_slot):
            carry = step(n_slot * i + slot, slot, carry)
        return carry

    acc_ref[...] = jnp.zeros_like(acc_ref)
    p_ref[n_slot - 1] = jnp.zeros(p_ref.shape[1:], p_ref.dtype)
    s0 = scores(0)
    s_ref[0] = s0
    m0 = jnp.full((1, tq), -jnp.inf, F32)
    carry0 = (m0, jnp.ones((1, tq), F32), jnp.max(s0, axis=0, keepdims=True))
    _, alpha, _ = lax.fori_loop(0, n_k // n_slot, body, carry0)
    accumulate(n_k - 1, n_slot - 1, alpha)
    l = acc_ref[V_HEAD_DIM:V_HEAD_DIM + 1, :]
    o_ref[...] = (acc_ref[:V_HEAD_DIM, :] / l).T


def _flash(qt, k, vt):
    _, s, _ = k.shape
    tq, tk = min(FLASH_TQ, s), vt.shape[3]
    return pl.pallas_call(
        _flash_kernel,
        grid=(MLA_HEADS, s // tq),
        in_specs=[pl.BlockSpec((1, MLA_QK_DIM, tq), lambda h, i: (h, 0, i)),
                  pl.BlockSpec((1, s, MLA_QK_DIM), lambda h, i: (h, 0, 0)),
                  pl.BlockSpec((1,) + vt.shape[1:], lambda h, i: (h, 0, 0, 0))],
        out_specs=pl.BlockSpec((tq, V_HEAD_DIM), lambda h, i: (i, h)),
        out_shape=jax.ShapeDtypeStruct((s, MLA_WIDTH), F32),
        scratch_shapes=[pltpu.VMEM((V_HEAD_DIM + V_ONES_ROWS, tq), F32),
                        pltpu.VMEM((FLASH_SLOTS, tk, tq), F32),
                        pltpu.VMEM((FLASH_SLOTS, tk, tq), BF16)],
        compiler_params=_params(("parallel", "arbitrary"),
                                2 * s * 256 * 2, 2 * s * V_HEAD_DIM * 2, FLASH_SLOTS * tk * tq * 6),
        name="mla_flash",
    )(qt, k, vt)


OUTPROJ_TM = 256


def _outproj_kernel(na_ref, mla_ref, x_ref, gna_ref, gmla_ref, gt_ref, w_ref, o_ref):
    na = _rms(na_ref[...], gna_ref[...]).astype(BF16)
    mla = _rms(mla_ref[...], gmla_ref[...]).astype(BF16)
    y = (jnp.dot(na, w_ref[:NA_WIDTH, :], preferred_element_type=F32)
         + jnp.dot(mla, w_ref[NA_WIDTH:, :], preferred_element_type=F32))
    o_ref[...] = x_ref[...] + gt_ref[...] * y


def _outproj(o_na, o_mla, x, g_na, g_mla, gt, w):
    s, d = x.shape
    tm = min(OUTPROJ_TM, s)
    const = lambda i: (0, 0)
    return pl.pallas_call(
        _outproj_kernel,
        grid=(s // tm,),
        in_specs=[pl.BlockSpec((tm, NA_WIDTH), lambda i: (i, 0)),
                  pl.BlockSpec((tm, MLA_WIDTH), lambda i: (i, 0)),
                  pl.BlockSpec((tm, d), lambda i: (i, 0)),
                  pl.BlockSpec((1, NA_WIDTH), const), pl.BlockSpec((1, MLA_WIDTH), const),
                  pl.BlockSpec((1, d), const), pl.BlockSpec(w.shape, const)],
        out_specs=pl.BlockSpec((tm, d), lambda i: (i, 0)),
        out_shape=jax.ShapeDtypeStruct((s, d), F32),
        compiler_params=_params(("parallel",), 2 * w.size * 2, 2 * 3 * tm * d * 4),
        name="outproj",
    )(o_na, o_mla, x, g_na, g_mla, gt, w)


FFN_TM = 512
FFN_TF = 512


def _ffn_kernel(x_ref, g_ref, sc_ref, sh_ref, gt_ref, gfin_ref, wg_ref, wu_ref, wd_ref, o_ref,
                hn_ref, acc_ref):
    j = pl.program_id(1)

    @pl.when(j == 0)
    def _():
        h = _rms(x_ref[...], g_ref[...]) * (1.0 + sc_ref[...]) + sh_ref[...]
        hn_ref[...] = h.astype(BF16)
        acc_ref[...] = jnp.zeros_like(acc_ref)

    h = hn_ref[...]
    a = jnp.dot(h, wg_ref[...], preferred_element_type=F32)
    b = jnp.dot(h, wu_ref[...], preferred_element_type=F32)
    t = (a * jax.nn.sigmoid(a) * b).astype(BF16)
    acc_ref[...] += jnp.dot(t, wd_ref[...], preferred_element_type=F32)

    @pl.when(j == pl.num_programs(1) - 1)
    def _():
        o_ref[...] = _rms(x_ref[...] + gt_ref[...] * acc_ref[...], gfin_ref[...])


def _ffn(x, g, sc, sh, gt, g_final, wg, wu, wd):
    s, d = x.shape
    tm, tf = min(FFN_TM, s), FFN_TF
    const = lambda i, j: (0, 0)
    vec = pl.BlockSpec((1, d), const)
    return pl.pallas_call(
        _ffn_kernel,
        grid=(s // tm, D_FF // tf),
        in_specs=[pl.BlockSpec((tm, d), lambda i, j: (i, 0)), vec, vec, vec, vec, vec,
                  pl.BlockSpec((d, tf), lambda i, j: (0, j)),
                  pl.BlockSpec((d, tf), lambda i, j: (0, j)),
                  pl.BlockSpec((tf, d), lambda i, j: (j, 0))],
        out_specs=pl.BlockSpec((tm, d), lambda i, j: (i, 0)),
        out_shape=jax.ShapeDtypeStruct((s, d), F32),
        scratch_shapes=[pltpu.VMEM((tm, d), BF16), pltpu.VMEM((tm, d), F32)],
        compiler_params=_params(("parallel", "arbitrary"),
                                4 * tm * d * 4, tm * d * 6, 2 * 3 * d * tf * 2, 3 * tm * tf * 4),
        name="ffn",
    )(x, g, sc, sh, gt, g_final, wg, wu, wd)


def _rotate_half_cols(w):
    half = QK_ROPE_DIM // 2
    return jnp.concatenate([-w[..., half:], w[..., :half]], axis=-1)


def _rope_tables(s):
    pos = jnp.arange(s, dtype=F32)
    inv_freq = ROPE_THETA ** (-jnp.arange(0, QK_ROPE_DIM, 2, dtype=F32) / QK_ROPE_DIM)
    ang = pos[:, None] * inv_freq[None, :]
    ang = jnp.concatenate([ang, ang], axis=-1)
    cos, sin = jnp.cos(ang), jnp.sin(ang)
    return (jnp.concatenate([cos, cos], axis=-1), jnp.concatenate([sin, sin], axis=-1),
            jnp.concatenate([cos, sin], axis=-1))


def _layer_weights(l, w_in, w_uq, w_ukv, w_o, w_gate, w_up, w_down):
    k_pe0 = 3 * NA_WIDTH + Q_LORA_RANK + KV_LORA_RANK
    w_in_l = w_in[l]
    w_in_ext = jnp.concatenate([w_in_l, _rotate_half_cols(w_in_l[:, k_pe0:])], axis=1).astype(BF16)
    uq = w_uq[l].reshape(Q_LORA_RANK, MLA_HEADS, MLA_QK_DIM)
    wqn = uq[:, :, :QK_NOPE_DIM].reshape(Q_LORA_RANK, -1).astype(BF16)
    wqp = uq[:, :, QK_NOPE_DIM:]
    wqr = _rotate_half_cols(wqp).reshape(Q_LORA_RANK, -1).astype(BF16)
    wqp = wqp.reshape(Q_LORA_RANK, -1).astype(BF16)
    ukv = w_ukv[l].reshape(KV_LORA_RANK, MLA_HEADS, QK_NOPE_DIM + V_HEAD_DIM)
    wkn = ukv[:, :, :QK_NOPE_DIM].reshape(KV_LORA_RANK, -1).astype(BF16)
    wv = ukv[:, :, QK_NOPE_DIM:].reshape(KV_LORA_RANK, -1).astype(BF16)
    return dict(w_in=w_in_ext, wqn=wqn, wqp=wqp, wqr=wqr, wkn=wkn, wv=wv,
                w_o=w_o[l].astype(BF16), w_gate=w_gate[l].astype(BF16),
                w_up=w_up[l].astype(BF16), w_down=w_down[l].astype(BF16))


def _encoder_layer(x, mod, lw, pt, tables, g_attn, g_q, g_kv, g_out_na, g_out_mla, g_ffn, g_final):
    d = D_MODEL
    sh1, sc1, gt1, sh2, sc2, gt2 = [mod[:, i * d:(i + 1) * d] for i in range(6)]
    proj = _inproj(x, g_attn, sc1, sh1, lw["w_in"])
    o_na = _na_attention(proj, pt)
    q, k, v = _mla_prep(proj, g_q, g_kv, lw["wqn"], lw["wqp"], lw["wqr"], lw["wkn"], lw["wv"], *tables)
    o_mla = _flash(q, k, v)
    x = _outproj(o_na, o_mla, x, g_out_na, g_out_mla, gt1, lw["w_o"])
    return _ffn(x, g_ffn, sc2, sh2, gt2, g_final, lw["w_gate"], lw["w_up"], lw["w_down"])


def kernel(x_prompt, x_sample, c_prompt, c_sample, w_ada, b_ada, g_attn, w_in, rpb, g_q, w_uq,
           g_kv, w_ukv, g_out_na, g_out_mla, w_o, g_ffn, w_gate, w_up, w_down, g_final):
    assert DEPTH == 1 and w_ada.shape[0] == DEPTH
    assert x_prompt.shape[0] == 1 and x_sample.shape[0] == 1
    l = 0
    c_cols = jnp.concatenate([c_prompt, c_sample], axis=0).T
    mod = _adaln(c_cols, w_ada[l], b_ada[l][None, :])
    lw = _layer_weights(l, w_in, w_uq, w_ukv, w_o, w_gate, w_up, w_down)
    pt = _na_bias(rpb[l])
    row = lambda a: a[None, :]
    outs = []
    for b, x in enumerate((x_prompt, x_sample)):
        y = _encoder_layer(x[0], mod[b:b + 1], lw, pt, _rope_tables(x.shape[1]),
                           row(g_attn[l]), row(g_q[l]), row(g_kv[l]), row(g_out_na[l]),
                           row(g_out_mla[l]), row(g_ffn[l]), row(g_final))
        outs.append(y[None])
    return tuple(outs)
```

```python
import functools

import jax
import jax.numpy as jnp
from jax import lax
from jax.experimental import pallas as pl
from jax.experimental.pallas import tpu as pltpu

D_MODEL = 2048
DEPTH = 1
GRID_W = 64
HEAD_DIM = 128
NA_HEADS = 8
NA_WIDTH = NA_HEADS * HEAD_DIM
NA_KH = 8
NA_KW = 16
MLA_HEADS = 8
Q_LORA_RANK = 512
KV_LORA_RANK = 512
QK_NOPE_DIM = 128
QK_ROPE_DIM = 64
V_HEAD_DIM = 128
MLA_QK_DIM = QK_NOPE_DIM + QK_ROPE_DIM
MLA_WIDTH = MLA_HEADS * V_HEAD_DIM
D_FF = 5632
ROPE_THETA = 10000.0
RMS_EPS = 1e-6

PROJ_WIDTH = 3 * NA_WIDTH + Q_LORA_RANK + KV_LORA_RANK + 2 * QK_ROPE_DIM

V7X_VMEM_BYTES = 64 * 1024 * 1024
V7X_LANES = 128
NEG = -1e30
LOG2_E = 1.4426950408889634

BF16 = jnp.bfloat16
F32 = jnp.float32
_NT = (((1,), (1,)), ((), ()))


def _vmem_limit(*buffer_bytes):
    need = int(sum(buffer_bytes) * 1.5) + (4 << 20)
    return min(max(need, 32 << 20), V7X_VMEM_BYTES - (8 << 20))


def _params(semantics, *buffer_bytes):
    return pltpu.CompilerParams(dimension_semantics=semantics,
                                vmem_limit_bytes=_vmem_limit(*buffer_bytes))


def _rms(x, g):
    return x * lax.rsqrt(jnp.mean(x * x, axis=-1, keepdims=True) + RMS_EPS) * g


ADA_TN = 512


def _adaln_kernel(c_ref, w_ref, b_ref, o_ref):
    c = c_ref[...]
    cs = c * jax.nn.sigmoid(c)
    w = w_ref[...]
    r0 = jnp.sum(w * cs[:, 0:1], axis=0, keepdims=True)
    r1 = jnp.sum(w * cs[:, 1:2], axis=0, keepdims=True)
    o_ref[...] = jnp.concatenate([r0, r1], axis=0) + b_ref[...]


def _adaln(c_cols, w, b):
    d, n = w.shape
    return pl.pallas_call(
        _adaln_kernel,
        grid=(n // ADA_TN,),
        in_specs=[pl.BlockSpec((d, 2), lambda j: (0, 0)),
                  pl.BlockSpec((d, ADA_TN), lambda j: (0, j)),
                  pl.BlockSpec((1, ADA_TN), lambda j: (0, j))],
        out_specs=pl.BlockSpec((2, ADA_TN), lambda j: (0, j)),
        out_shape=jax.ShapeDtypeStruct((2, n), F32),
        compiler_params=_params(("arbitrary",), 2 * d * ADA_TN * 4, 2 * d * V7X_LANES * 4),
        name="adaln",
    )(c_cols, w, b)


INPROJ_TM = 1024
INPROJ_TN = PROJ_WIDTH // 3


def _inproj_kernel(x_ref, g_ref, sc_ref, sh_ref, w_ref, o_ref, hn_ref):
    @pl.when(pl.program_id(1) == 0)
    def _():
        h = _rms(x_ref[...], g_ref[...]) * (1.0 + sc_ref[...]) + sh_ref[...]
        hn_ref[...] = h.astype(BF16)

    o_ref[...] = jnp.dot(hn_ref[...], w_ref[...], preferred_element_type=F32).astype(o_ref.dtype)


def _inproj(x, g, sc, sh, w):
    s, d = x.shape
    tm, tn = min(INPROJ_TM, s), INPROJ_TN
    row = lambda i, j: (0, 0)
    return pl.pallas_call(
        _inproj_kernel,
        grid=(s // tm, PROJ_WIDTH // tn),
        in_specs=[pl.BlockSpec((tm, d), lambda i, j: (i, 0)),
                  pl.BlockSpec((1, d), row), pl.BlockSpec((1, d), row), pl.BlockSpec((1, d), row),
                  pl.BlockSpec((d, tn), lambda i, j: (0, j))],
        out_specs=pl.BlockSpec((tm, tn), lambda i, j: (i, j)),
        out_shape=jax.ShapeDtypeStruct((s, PROJ_WIDTH), BF16),
        scratch_shapes=[pltpu.VMEM((tm, d), BF16)],
        compiler_params=_params(("parallel", "arbitrary"),
                                2 * tm * d * 4, tm * d * 2, 2 * d * tn * 2, 2 * tm * tn * 2),
        name="inproj",
    )(x, g, sc, sh, w)


NA_G = 8
NA_KR = NA_G + NA_KH
NA_KBLK = 4
NA_NKB = NA_KR // NA_KBLK
NA_HPS = 2
NA_PT_LEFT_MASKED = 14
NA_PT_RIGHT_MASKED = 15
NA_PT_ENTRIES = 16
RPB_ROWS = 2 * NA_KH - 1
RPB_COLS = 2 * NA_KW - 1


def _na_bias_kernel(rpb_ref, o_ref):
    h = pl.program_id(0)
    qc = lax.broadcasted_iota(jnp.int32, (GRID_W, 2 * GRID_W), 0)
    kk = lax.broadcasted_iota(jnp.int32, (GRID_W, 2 * GRID_W), 1)
    right = kk >= GRID_W
    kc = jnp.where(right, kk - GRID_W, kk)
    dc = kc - qc + (NA_KW - 1)
    c0 = jnp.clip(qc - NA_KW // 2, 0, GRID_W - NA_KW)
    col_ok = (kc >= c0) & (kc < c0 + NA_KW)

    def tile(dr_left, dr_right):
        acc = jnp.full((GRID_W, 2 * GRID_W), NEG, F32)
        for d in range(RPB_COLS):
            base = h * (RPB_ROWS * RPB_COLS) + d
            left = NEG if dr_left is None else rpb_ref[base + dr_left * RPB_COLS]
            rght = NEG if dr_right is None else rpb_ref[base + dr_right * RPB_COLS]
            acc = jnp.where(dc == d, jnp.where(right, rght, left), acc)
        return jnp.where(col_ok, acc, NEG)

    for e in range(NA_PT_LEFT_MASKED):
        o_ref[0, e] = tile(e, e + 1)
    o_ref[0, NA_PT_LEFT_MASKED] = tile(None, 3)
    o_ref[0, NA_PT_RIGHT_MASKED] = tile(10, None)


def _na_bias(rpb):
    return pl.pallas_call(
        _na_bias_kernel,
        grid=(NA_HEADS,),
        in_specs=[pl.BlockSpec(memory_space=pltpu.SMEM)],
        out_specs=pl.BlockSpec((1, NA_PT_ENTRIES, GRID_W, 2 * GRID_W), lambda h: (h, 0, 0, 0)),
        out_shape=jax.ShapeDtypeStruct((NA_HEADS, NA_PT_ENTRIES, GRID_W, 2 * GRID_W), F32),
        compiler_params=_params(("arbitrary",)),
        name="na_bias",
    )(rpb.reshape(-1))


def _na_row_plan(q_off):
    plan = []
    for i in range(NA_G):
        qr = q_off + i
        rs = min(max(qr - NA_KH // 2, 0), NA_KR - NA_KH)
        p_lo, p_hi = rs // 2, (rs + NA_KH - 1) // 2
        entries = []
        for p in range(p_lo, p_hi + 1):
            ok_l = rs <= 2 * p < rs + NA_KH
            ok_r = rs <= 2 * p + 1 < rs + NA_KH
            dr0 = 2 * p - qr + (NA_KH - 1)
            if ok_l and ok_r:
                assert 0 <= dr0 < NA_PT_LEFT_MASKED
                entries.append(dr0)
            elif ok_r:
                assert dr0 + 1 == 3
                entries.append(NA_PT_LEFT_MASKED)
            else:
                assert ok_l and dr0 == 10
                entries.append(NA_PT_RIGHT_MASKED)
        plan.append((p_lo, tuple(entries)))
    return plan


def _na_kernel(q_ref, k0, k1, k2, k3, v0, v1, v2, v3, pt_ref, o_ref, s_ref, p_ref, l_ref):
    g, hp = pl.program_id(0), pl.program_id(1)
    last = pl.num_programs(0) - 1
    scale = HEAD_DIM ** -0.5

    def attend(q_off):
        plan = _na_row_plan(q_off)
        for hh in range(NA_HPS):
            lanes = slice(hh * HEAD_DIM, (hh + 1) * HEAD_DIM)
            q = (q_ref[:, lanes].astype(F32) * scale).astype(BF16)
            k = jnp.concatenate([r[:, lanes] for r in (k0, k1, k2, k3)], axis=0)
            s_ref[hh] = lax.dot_general(q, k, _NT, preferred_element_type=F32)
            p_ref[hh] = jnp.zeros(p_ref.shape[1:], p_ref.dtype)
        for hh in range(NA_HPS):
            h = hp * NA_HPS + hh
            for i, (p_lo, entries) in enumerate(plan):
                rows = slice(i * GRID_W, (i + 1) * GRID_W)
                cols = slice(p_lo * 2 * GRID_W, (p_lo + len(entries)) * 2 * GRID_W)
                bias = jnp.concatenate([pt_ref[h, e] for e in entries], axis=1)
                s = s_ref[hh, rows, cols] + bias
                p = jnp.exp(s - jnp.max(s, axis=-1, keepdims=True))
                l_ref[hh, rows, :] = jnp.sum(p, axis=-1, keepdims=True)
                p_ref[hh, rows, cols] = p.astype(BF16)
        for hh in range(NA_HPS):
            lanes = slice(hh * HEAD_DIM, (hh + 1) * HEAD_DIM)
            v = jnp.concatenate([r[:, lanes] for r in (v0, v1, v2, v3)], axis=0)
            o = jnp.dot(p_ref[hh], v, preferred_element_type=F32)
            o_ref[:, lanes] = o / l_ref[hh]

    pl.when(g == 0)(functools.partial(attend, 0))
    pl.when((g > 0) & (g < last))(functools.partial(attend, NA_KH // 2))
    pl.when(g == last)(functools.partial(attend, NA_KH))


def _na_attention(proj, pt):
    s = proj.shape[0]
    rows = s // GRID_W
    assert rows % NA_G == 0 and rows >= 2 * NA_G
    tq, tkb = NA_G * GRID_W, NA_KBLK * GRID_W
    n_kb = rows // NA_KBLK
    width = NA_HPS * HEAD_DIM
    k_col, v_col = NA_HEADS // NA_HPS, 2 * NA_HEADS // NA_HPS

    def kv_spec(j, col0):
        def index(g, hp):
            first = jnp.clip(g * (NA_G // NA_KBLK) - 1, 0, n_kb - NA_NKB)
            return (first + j, col0 + hp)
        return pl.BlockSpec((tkb, width), index)

    return pl.pallas_call(
        _na_kernel,
        grid=(rows // NA_G, NA_HEADS // NA_HPS),
        in_specs=([pl.BlockSpec((tq, width), lambda g, hp: (g, hp))]
                  + [kv_spec(j, k_col) for j in range(NA_NKB)]
                  + [kv_spec(j, v_col) for j in range(NA_NKB)]
                  + [pl.BlockSpec(pt.shape, lambda g, hp: (0, 0, 0, 0))]),
        out_specs=pl.BlockSpec((tq, width), lambda g, hp: (g, hp)),
        out_shape=jax.ShapeDtypeStruct((s, NA_WIDTH), F32),
        scratch_shapes=[pltpu.VMEM((NA_HPS, tq, NA_KR * GRID_W), F32),
                        pltpu.VMEM((NA_HPS, tq, NA_KR * GRID_W), BF16),
                        pltpu.VMEM((NA_HPS, tq, 1), F32)],
        compiler_params=_params(("parallel", "arbitrary"),
                                2 * pt.size * 4, NA_HPS * tq * NA_KR * GRID_W * 6,
                                NA_HPS * tq * V7X_LANES * 4, 2 * 3 * tq * width * 4),
        name="na_attention",
    )(proj, *([proj] * (2 * NA_NKB)), pt)


MLAPREP_TM = 512
V_ONES_ROWS = 16


def _mla_prep_kernel(qc_ref, kvc_ref, kpe_ref, gq_ref, gkv_ref, wqn_ref, wqp_ref, wqr_ref,
                     wkn_ref, wv_ref, cc_ref, ss_ref, cs_ref, q_ref, k_ref, v_ref):
    scale = MLA_QK_DIM ** -0.5 * LOG2_E
    qn = _rms(qc_ref[...].astype(F32), gq_ref[...]).astype(BF16)
    kvn = _rms(kvc_ref[...].astype(F32), gkv_ref[...]).astype(BF16)

    prod = kpe_ref[...].astype(F32) * cs_ref[...]
    k_pe = (prod + pltpu.roll(prod, QK_ROPE_DIM, 1))[:, :QK_ROPE_DIM].astype(BF16)

    cos = jnp.concatenate([cc_ref[...]] * (MLA_HEADS // 2), axis=1)
    sin = jnp.concatenate([ss_ref[...]] * (MLA_HEADS // 2), axis=1)
    q_pe = (jnp.dot(qn, wqp_ref[...], preferred_element_type=F32) * cos
            + jnp.dot(qn, wqr_ref[...], preferred_element_type=F32) * sin)
    q_pe_t = (q_pe * scale).T.astype(BF16)

    q_nope = jnp.dot(qn, wqn_ref[...], preferred_element_type=F32) * scale
    q_nope_t = q_nope.T.astype(BF16)
    k_nope = jnp.dot(kvn, wkn_ref[...], preferred_element_type=F32).astype(BF16)
    v_t = jnp.dot(kvn, wv_ref[...], preferred_element_type=F32).T.astype(BF16)
    for h in range(MLA_HEADS):
        nope = slice(h * QK_NOPE_DIM, (h + 1) * QK_NOPE_DIM)
        q_ref[h, :QK_NOPE_DIM, :] = q_nope_t[nope, :]
        q_ref[h, QK_NOPE_DIM:MLA_QK_DIM, :] = q_pe_t[h * QK_ROPE_DIM:(h + 1) * QK_ROPE_DIM, :]
        k_ref[h, :, :QK_NOPE_DIM] = k_nope[:, nope]
        k_ref[h, :, QK_NOPE_DIM:MLA_QK_DIM] = k_pe
        if q_ref.shape[1] > MLA_QK_DIM:
            q_ref[h, MLA_QK_DIM:, :] = jnp.zeros((q_ref.shape[1] - MLA_QK_DIM, q_ref.shape[2]), BF16)
            k_ref[h, :, MLA_QK_DIM:] = jnp.zeros((k_ref.shape[1], k_ref.shape[2] - MLA_QK_DIM), BF16)
        v_ref[h, 0, :V_HEAD_DIM, :] = v_t[h * V_HEAD_DIM:(h + 1) * V_HEAD_DIM, :]
        v_ref[h, 0, V_HEAD_DIM:, :] = jnp.ones((V_ONES_ROWS, v_t.shape[1]), BF16)


def _mla_prep(proj, g_q, g_kv, wqn, wqp, wqr, wkn, wv, cc, ss, cs, qk_width):
    s = proj.shape[0]
    tm = min(MLAPREP_TM, s)
    qc_blk = 3 * NA_WIDTH // Q_LORA_RANK
    kpe_blk = (3 * NA_WIDTH + Q_LORA_RANK + KV_LORA_RANK) // (2 * QK_ROPE_DIM)
    const = lambda i: (0, 0)
    return pl.pallas_call(
        _mla_prep_kernel,
        grid=(s // tm,),
        in_specs=[pl.BlockSpec((tm, Q_LORA_RANK), lambda i: (i, qc_blk)),
                  pl.BlockSpec((tm, KV_LORA_RANK), lambda i: (i, qc_blk + 1)),
                  pl.BlockSpec((tm, 2 * QK_ROPE_DIM), lambda i: (i, kpe_blk)),
                  pl.BlockSpec((1, Q_LORA_RANK), const), pl.BlockSpec((1, KV_LORA_RANK), const),
                  pl.BlockSpec(wqn.shape, const), pl.BlockSpec(wqp.shape, const),
                  pl.BlockSpec(wqr.shape, const), pl.BlockSpec(wkn.shape, const),
                  pl.BlockSpec(wv.shape, const),
                  pl.BlockSpec((tm, 2 * QK_ROPE_DIM), lambda i: (i, 0)),
                  pl.BlockSpec((tm, 2 * QK_ROPE_DIM), lambda i: (i, 0)),
                  pl.BlockSpec((tm, 2 * QK_ROPE_DIM), lambda i: (i, 0))],
        out_specs=[pl.BlockSpec((MLA_HEADS, qk_width, tm), lambda i: (0, 0, i)),
                   pl.BlockSpec((MLA_HEADS, tm, qk_width), lambda i: (0, i, 0)),
                   pl.BlockSpec((MLA_HEADS, 1, V_HEAD_DIM + V_ONES_ROWS, tm), lambda i: (0, i, 0, 0))],
        out_shape=[jax.ShapeDtypeStruct((MLA_HEADS, qk_width, s), BF16),
                   jax.ShapeDtypeStruct((MLA_HEADS, s, qk_width), BF16),
                   jax.ShapeDtypeStruct((MLA_HEADS, s // tm, V_HEAD_DIM + V_ONES_ROWS, tm), BF16)],
        compiler_params=_params(("parallel",), 2 * 2 * 512 * 4096, 2 * 3 * MLA_HEADS * tm * 256 * 2,
                                2 * tm * 1536 * 4, 8 * tm * 1024 * 4),
        name="mla_prep",
    )(proj, proj, proj, g_q, g_kv, wqn, wqp, wqr, wkn, wv, cc, ss, cs)


FLASH_TQ = 512


def _flash_kernel(qt_ref, k_ref, vt_ref, o_ref, acc_ref, s_ref, p_ref):
    tq = qt_ref.shape[2]
    n_slot, tk, _ = s_ref.shape
    tv = vt_ref.shape[3]
    n_sub = tk // tv
    n_k = vt_ref.shape[1] // n_sub
    assert n_k % n_slot == 0

    def scores(j):
        k = k_ref[0, pl.ds(pl.multiple_of(j * tk, tk), tk), :]
        return jnp.dot(k, qt_ref[0], preferred_element_type=F32)

    def accumulate(j, slot, alpha):
        pv = sum(jnp.dot(vt_ref[0, j * n_sub + u], p_ref[slot, u * tv:(u + 1) * tv, :],
                         preferred_element_type=F32) for u in range(n_sub))
        acc_ref[...] = alpha * acc_ref[...] + pv

    def step(j, slot, carry):
        m, alpha_prev, tile_max = carry
        s_next = scores(jnp.minimum(j + 1, n_k - 1))
        s_ref[(slot + 1) % n_slot] = s_next
        tile_max_next = jnp.max(s_next, axis=0, keepdims=True)
        accumulate(jnp.maximum(j - 1, 0), (slot - 1) % n_slot, alpha_prev)
        m_new = jnp.maximum(m, tile_max)
        alpha = jnp.exp2(m - m_new)
        p_ref[slot] = jnp.exp2((s_ref[slot] - m_new).astype(BF16))
        return m_new, alpha, tile_max_next

    def body(i, carry):
        for slot in range(n_slot):
            carry = step(n_slot * i + slot, slot, carry)
        return carry

    acc_ref[...] = jnp.zeros_like(acc_ref)
    p_ref[n_slot - 1] = jnp.zeros(p_ref.shape[1:], p_ref.dtype)
    s0 = scores(0)
    s_ref[0] = s0
    m0 = jnp.full((1, tq), -jnp.inf, F32)
    carry0 = (m0, jnp.ones((1, tq), F32), jnp.max(s0, axis=0, keepdims=True))
    _, alpha, _ = lax.fori_loop(0, n_k // n_slot, body, carry0)
    accumulate(n_k - 1, n_slot - 1, alpha)
    l = acc_ref[V_HEAD_DIM:V_HEAD_DIM + 1, :]
    o_ref[...] = (acc_ref[:V_HEAD_DIM, :] / l).T


def _flash_config(s):
    if s >= 16384:
        return 256, 512, 8
    return MLA_QK_DIM, min(1024, s // 4), 4


def _flash(qt, k, vt, tk, n_slot):
    _, s, qk_width = k.shape
    tq = min(FLASH_TQ, s)
    return pl.pallas_call(
        _flash_kernel,
        grid=(MLA_HEADS, s // tq),
        in_specs=[pl.BlockSpec((1, qk_width, tq), lambda h, i: (h, 0, i)),
                  pl.BlockSpec((1, s, qk_width), lambda h, i: (h, 0, 0)),
                  pl.BlockSpec((1,) + vt.shape[1:], lambda h, i: (h, 0, 0, 0))],
        out_specs=pl.BlockSpec((tq, V_HEAD_DIM), lambda h, i: (i, h)),
        out_shape=jax.ShapeDtypeStruct((s, MLA_WIDTH), F32),
        scratch_shapes=[pltpu.VMEM((V_HEAD_DIM + V_ONES_ROWS, tq), F32),
                        pltpu.VMEM((n_slot, tk, tq), F32),
                        pltpu.VMEM((n_slot, tk, tq), BF16)],
        compiler_params=_params(("parallel", "arbitrary"),
                                2 * s * 256 * 2, 2 * s * V_HEAD_DIM * 2, n_slot * tk * tq * 6),
        name="mla_flash",
    )(qt, k, vt)


OUTPROJ_TM = 256


def _outproj_kernel(na_ref, mla_ref, x_ref, gna_ref, gmla_ref, gt_ref, w_ref, o_ref):
    na = _rms(na_ref[...], gna_ref[...]).astype(BF16)
    mla = _rms(mla_ref[...], gmla_ref[...]).astype(BF16)
    y = (jnp.dot(na, w_ref[:NA_WIDTH, :], preferred_element_type=F32)
         + jnp.dot(mla, w_ref[NA_WIDTH:, :], preferred_element_type=F32))
    o_ref[...] = x_ref[...] + gt_ref[...] * y


def _outproj(o_na, o_mla, x, g_na, g_mla, gt, w):
    s, d = x.shape
    tm = min(OUTPROJ_TM, s)
    const = lambda i: (0, 0)
    return pl.pallas_call(
        _outproj_kernel,
        grid=(s // tm,),
        in_specs=[pl.BlockSpec((tm, NA_WIDTH), lambda i: (i, 0)),
                  pl.BlockSpec((tm, MLA_WIDTH), lambda i: (i, 0)),
                  pl.BlockSpec((tm, d), lambda i: (i, 0)),
                  pl.BlockSpec((1, NA_WIDTH), const), pl.BlockSpec((1, MLA_WIDTH), const),
                  pl.BlockSpec((1, d), const), pl.BlockSpec(w.shape, const)],
        out_specs=pl.BlockSpec((tm, d), lambda i: (i, 0)),
        out_shape=jax.ShapeDtypeStruct((s, d), F32),
        compiler_params=_params(("parallel",), 2 * w.size * 2, 2 * 3 * tm * d * 4),
        name="outproj",
    )(o_na, o_mla, x, g_na, g_mla, gt, w)


FFN_TM = 512
FFN_TF = 512


def _ffn_kernel(x_ref, g_ref, sc_ref, sh_ref, gt_ref, gfin_ref, wg_ref, wu_ref, wd_ref, o_ref,
                hn_ref, acc_ref):
    j = pl.program_id(1)

    @pl.when(j == 0)
    def _():
        h = _rms(x_ref[...], g_ref[...]) * (1.0 + sc_ref[...]) + sh_ref[...]
        hn_ref[...] = h.astype(BF16)
        acc_ref[...] = jnp.zeros_like(acc_ref)

    h = hn_ref[...]
    a = jnp.dot(h, wg_ref[...], preferred_element_type=F32)
    b = jnp.dot(h, wu_ref[...], preferred_element_type=F32)
    t = (a * jax.nn.sigmoid(a) * b).astype(BF16)
    acc_ref[...] += jnp.dot(t, wd_ref[...], preferred_element_type=F32)

    @pl.when(j == pl.num_programs(1) - 1)
    def _():
        o_ref[...] = _rms(x_ref[...] + gt_ref[...] * acc_ref[...], gfin_ref[...])


def _ffn(x, g, sc, sh, gt, g_final, wg, wu, wd):
    s, d = x.shape
    tm, tf = min(FFN_TM, s), FFN_TF
    const = lambda i, j: (0, 0)
    vec = pl.BlockSpec((1, d), const)
    return pl.pallas_call(
        _ffn_kernel,
        grid=(s // tm, D_FF // tf),
        in_specs=[pl.BlockSpec((tm, d), lambda i, j: (i, 0)), vec, vec, vec, vec, vec,
                  pl.BlockSpec((d, tf), lambda i, j: (0, j)),
                  pl.BlockSpec((d, tf), lambda i, j: (0, j)),
                  pl.BlockSpec((tf, d), lambda i, j: (j, 0))],
        out_specs=pl.BlockSpec((tm, d), lambda i, j: (i, 0)),
        out_shape=jax.ShapeDtypeStruct((s, d), F32),
        scratch_shapes=[pltpu.VMEM((tm, d), BF16), pltpu.VMEM((tm, d), F32)],
        compiler_params=_params(("parallel", "arbitrary"),
                                4 * tm * d * 4, tm * d * 6, 2 * 3 * d * tf * 2, 3 * tm * tf * 4),
        name="ffn",
    )(x, g, sc, sh, gt, g_final, wg, wu, wd)


def _rotate_half_cols(w):
    half = QK_ROPE_DIM // 2
    return jnp.concatenate([-w[..., half:], w[..., :half]], axis=-1)


def _rope_tables(s):
    pos = jnp.arange(s, dtype=F32)
    inv_freq = ROPE_THETA ** (-jnp.arange(0, QK_ROPE_DIM, 2, dtype=F32) / QK_ROPE_DIM)
    ang = pos[:, None] * inv_freq[None, :]
    ang = jnp.concatenate([ang, ang], axis=-1)
    cos, sin = jnp.cos(ang), jnp.sin(ang)
    return (jnp.concatenate([cos, cos], axis=-1), jnp.concatenate([sin, sin], axis=-1),
            jnp.concatenate([cos, sin], axis=-1))


def _layer_weights(l, w_in, w_uq, w_ukv, w_o, w_gate, w_up, w_down):
    k_pe0 = 3 * NA_WIDTH + Q_LORA_RANK + KV_LORA_RANK
    w_in_l = w_in[l]
    w_in_ext = jnp.concatenate([w_in_l, _rotate_half_cols(w_in_l[:, k_pe0:])], axis=1).astype(BF16)
    uq = w_uq[l].reshape(Q_LORA_RANK, MLA_HEADS, MLA_QK_DIM)
    wqn = uq[:, :, :QK_NOPE_DIM].reshape(Q_LORA_RANK, -1).astype(BF16)
    wqp = uq[:, :, QK_NOPE_DIM:]
    wqr = _rotate_half_cols(wqp).reshape(Q_LORA_RANK, -1).astype(BF16)
    wqp = wqp.reshape(Q_LORA_RANK, -1).astype(BF16)
    ukv = w_ukv[l].reshape(KV_LORA_RANK, MLA_HEADS, QK_NOPE_DIM + V_HEAD_DIM)
    wkn = ukv[:, :, :QK_NOPE_DIM].reshape(KV_LORA_RANK, -1).astype(BF16)
    wv = ukv[:, :, QK_NOPE_DIM:].reshape(KV_LORA_RANK, -1).astype(BF16)
    return dict(w_in=w_in_ext, wqn=wqn, wqp=wqp, wqr=wqr, wkn=wkn, wv=wv,
                w_o=w_o[l].astype(BF16), w_gate=w_gate[l].astype(BF16),
                w_up=w_up[l].astype(BF16), w_down=w_down[l].astype(BF16))


def _encoder_layer(x, mod, lw, pt, tables, g_attn, g_q, g_kv, g_out_na, g_out_mla, g_ffn, g_final):
    d = D_MODEL
    sh1, sc1, gt1, sh2, sc2, gt2 = [mod[:, i * d:(i + 1) * d] for i in range(6)]
    proj = _inproj(x, g_attn, sc1, sh1, lw["w_in"])
    o_na = _na_attention(proj, pt)
    qk_width, flash_tk, flash_slots = _flash_config(x.shape[0])
    q, k, v = _mla_prep(proj, g_q, g_kv, lw["wqn"], lw["wqp"], lw["wqr"], lw["wkn"], lw["wv"], *tables,
                        qk_width=qk_width)
    o_mla = _flash(q, k, v, flash_tk, flash_slots)
    x = _outproj(o_na, o_mla, x, g_out_na, g_out_mla, gt1, lw["w_o"])
    return _ffn(x, g_ffn, sc2, sh2, gt2, g_final, lw["w_gate"], lw["w_up"], lw["w_down"])


def kernel(x_prompt, x_sample, c_prompt, c_sample, w_ada, b_ada, g_attn, w_in, rpb, g_q, w_uq,
           g_kv, w_ukv, g_out_na, g_out_mla, w_o, g_ffn, w_gate, w_up, w_down, g_final):
    assert DEPTH == 1 and w_ada.shape[0] == DEPTH
    assert x_prompt.shape[0] == 1 and x_sample.shape[0] == 1
    l = 0
    c_cols = jnp.concatenate([c_prompt, c_sample], axis=0).T
    mod = _adaln(c_cols, w_ada[l], b_ada[l][None, :])
    lw = _layer_weights(l, w_in, w_uq, w_ukv, w_o, w_gate, w_up, w_down)
    pt = _na_bias(rpb[l])
    row = lambda a: a[None, :]
    outs = []
    for b, x in enumerate((x_prompt, x_sample)):
        y = _encoder_layer(x[0], mod[b:b + 1], lw, pt, _rope_tables(x.shape[1]),
                           row(g_attn[l]), row(g_q[l]), row(g_kv[l]), row(g_out_na[l]),
                           row(g_out_mla[l]), row(g_ffn[l]), row(g_final))
        outs.append(y[None])
    return tuple(outs)
```

```python
import functools

import jax
import jax.numpy as jnp
from jax import lax
from jax.experimental import pallas as pl
from jax.experimental.pallas import tpu as pltpu

D_MODEL = 2048
DEPTH = 1
GRID_W = 64
HEAD_DIM = 128
NA_HEADS = 8
NA_WIDTH = NA_HEADS * HEAD_DIM
NA_KH = 8
NA_KW = 16
MLA_HEADS = 8
Q_LORA_RANK = 512
KV_LORA_RANK = 512
QK_NOPE_DIM = 128
QK_ROPE_DIM = 64
V_HEAD_DIM = 128
MLA_QK_DIM = QK_NOPE_DIM + QK_ROPE_DIM
MLA_WIDTH = MLA_HEADS * V_HEAD_DIM
D_FF = 5632
ROPE_THETA = 10000.0
RMS_EPS = 1e-6

PROJ_WIDTH = 3 * NA_WIDTH + Q_LORA_RANK + KV_LORA_RANK + 2 * QK_ROPE_DIM

V7X_VMEM_BYTES = 64 * 1024 * 1024
V7X_LANES = 128
NEG = -1e30
LOG2_E = 1.4426950408889634

BF16 = jnp.bfloat16
F32 = jnp.float32
_NT = (((1,), (1,)), ((), ()))


def _vmem_limit(*buffer_bytes):
    need = int(sum(buffer_bytes) * 1.5) + (4 << 20)
    return min(max(need, 32 << 20), V7X_VMEM_BYTES - (8 << 20))


def _params(semantics, *buffer_bytes):
    return pltpu.CompilerParams(dimension_semantics=semantics,
                                vmem_limit_bytes=_vmem_limit(*buffer_bytes))


def _rms(x, g):
    return x * lax.rsqrt(jnp.mean(x * x, axis=-1, keepdims=True) + RMS_EPS) * g


ADA_TN = 512


def _adaln_kernel(c_ref, w_ref, b_ref, o_ref):
    c = c_ref[...]
    cs = c * jax.nn.sigmoid(c)
    w = w_ref[...]
    r0 = jnp.sum(w * cs[:, 0:1], axis=0, keepdims=True)
    r1 = jnp.sum(w * cs[:, 1:2], axis=0, keepdims=True)
    o_ref[...] = jnp.concatenate([r0, r1], axis=0) + b_ref[...]


def _adaln(c_cols, w, b):
    d, n = w.shape
    return pl.pallas_call(
        _adaln_kernel,
        grid=(n // ADA_TN,),
        in_specs=[pl.BlockSpec((d, 2), lambda j: (0, 0)),
                  pl.BlockSpec((d, ADA_TN), lambda j: (0, j)),
                  pl.BlockSpec((1, ADA_TN), lambda j: (0, j))],
        out_specs=pl.BlockSpec((2, ADA_TN), lambda j: (0, j)),
        out_shape=jax.ShapeDtypeStruct((2, n), F32),
        compiler_params=_params(("arbitrary",), 2 * d * ADA_TN * 4, 2 * d * V7X_LANES * 4),
        name="adaln",
    )(c_cols, w, b)


INPROJ_TM = 1024
INPROJ_TN = PROJ_WIDTH // 3


def _inproj_kernel(x_ref, g_ref, sc_ref, sh_ref, w_ref, o_ref, hn_ref):
    @pl.when(pl.program_id(1) == 0)
    def _():
        h = _rms(x_ref[...], g_ref[...]) * (1.0 + sc_ref[...]) + sh_ref[...]
        hn_ref[...] = h.astype(BF16)

    o_ref[...] = jnp.dot(hn_ref[...], w_ref[...], preferred_element_type=F32).astype(o_ref.dtype)


def _inproj(x, g, sc, sh, w):
    s, d = x.shape
    tm, tn = min(INPROJ_TM, s), INPROJ_TN
    row = lambda i, j: (0, 0)
    return pl.pallas_call(
        _inproj_kernel,
        grid=(s // tm, PROJ_WIDTH // tn),
        in_specs=[pl.BlockSpec((tm, d), lambda i, j: (i, 0)),
                  pl.BlockSpec((1, d), row), pl.BlockSpec((1, d), row), pl.BlockSpec((1, d), row),
                  pl.BlockSpec((d, tn), lambda i, j: (0, j))],
        out_specs=pl.BlockSpec((tm, tn), lambda i, j: (i, j)),
        out_shape=jax.ShapeDtypeStruct((s, PROJ_WIDTH), BF16),
        scratch_shapes=[pltpu.VMEM((tm, d), BF16)],
        compiler_params=_params(("parallel", "arbitrary"),
                                2 * tm * d * 4, tm * d * 2, 2 * d * tn * 2, 2 * tm * tn * 2),
        name="inproj",
    )(x, g, sc, sh, w)


NA_G = 8
NA_KR = NA_G + NA_KH
NA_KBLK = 4
NA_NKB = NA_KR // NA_KBLK
NA_HPS = 2
NA_PT_LEFT_MASKED = 14
NA_PT_RIGHT_MASKED = 15
NA_PT_ENTRIES = 16
RPB_ROWS = 2 * NA_KH - 1
RPB_COLS = 2 * NA_KW - 1


def _na_bias_kernel(rpb_ref, o_ref):
    h = pl.program_id(0)
    qc = lax.broadcasted_iota(jnp.int32, (GRID_W, 2 * GRID_W), 0)
    kk = lax.broadcasted_iota(jnp.int32, (GRID_W, 2 * GRID_W), 1)
    right = kk >= GRID_W
    kc = jnp.where(right, kk - GRID_W, kk)
    dc = kc - qc + (NA_KW - 1)
    c0 = jnp.clip(qc - NA_KW // 2, 0, GRID_W - NA_KW)
    col_ok = (kc >= c0) & (kc < c0 + NA_KW)

    def tile(dr_left, dr_right):
        acc = jnp.full((GRID_W, 2 * GRID_W), NEG, F32)
        for d in range(RPB_COLS):
            base = h * (RPB_ROWS * RPB_COLS) + d
            left = NEG if dr_left is None else rpb_ref[base + dr_left * RPB_COLS]
            rght = NEG if dr_right is None else rpb_ref[base + dr_right * RPB_COLS]
            acc = jnp.where(dc == d, jnp.where(right, rght, left), acc)
        return jnp.where(col_ok, acc, NEG)

    for e in range(NA_PT_LEFT_MASKED):
        o_ref[0, e] = tile(e, e + 1)
    o_ref[0, NA_PT_LEFT_MASKED] = tile(None, 3)
    o_ref[0, NA_PT_RIGHT_MASKED] = tile(10, None)


def _na_bias(rpb):
    return pl.pallas_call(
        _na_bias_kernel,
        grid=(NA_HEADS,),
        in_specs=[pl.BlockSpec(memory_space=pltpu.SMEM)],
        out_specs=pl.BlockSpec((1, NA_PT_ENTRIES, GRID_W, 2 * GRID_W), lambda h: (h, 0, 0, 0)),
        out_shape=jax.ShapeDtypeStruct((NA_HEADS, NA_PT_ENTRIES, GRID_W, 2 * GRID_W), F32),
        compiler_params=_params(("arbitrary",)),
        name="na_bias",
    )(rpb.reshape(-1))


def _na_row_plan(q_off):
    plan = []
    for i in range(NA_G):
        qr = q_off + i
        rs = min(max(qr - NA_KH // 2, 0), NA_KR - NA_KH)
        p_lo, p_hi = rs // 2, (rs + NA_KH - 1) // 2
        entries = []
        for p in range(p_lo, p_hi + 1):
            ok_l = rs <= 2 * p < rs + NA_KH
            ok_r = rs <= 2 * p + 1 < rs + NA_KH
            dr0 = 2 * p - qr + (NA_KH - 1)
            if ok_l and ok_r:
                assert 0 <= dr0 < NA_PT_LEFT_MASKED
                entries.append(dr0)
            elif ok_r:
                assert dr0 + 1 == 3
                entries.append(NA_PT_LEFT_MASKED)
            else:
                assert ok_l and dr0 == 10
                entries.append(NA_PT_RIGHT_MASKED)
        plan.append((p_lo, tuple(entries)))
    return plan


def _na_kernel(q_ref, k0, k1, k2, k3, v0, v1, v2, v3, pt_ref, o_ref, s_ref, p_ref, l_ref, *, n_groups):
    g, hp = pl.program_id(0), pl.program_id(1)
    last = n_groups - 1
    scale = HEAD_DIM ** -0.5

    def attend(q_off):
        plan = _na_row_plan(q_off)
        for hh in range(NA_HPS):
            lanes = slice(hh * HEAD_DIM, (hh + 1) * HEAD_DIM)
            q = (q_ref[:, lanes].astype(F32) * scale).astype(BF16)
            k = jnp.concatenate([r[:, lanes] for r in (k0, k1, k2, k3)], axis=0)
            s_ref[hh] = lax.dot_general(q, k, _NT, preferred_element_type=F32)
            p_ref[hh] = jnp.zeros(p_ref.shape[1:], p_ref.dtype)
        for hh in range(NA_HPS):
            h = hp * NA_HPS + hh
            for i, (p_lo, entries) in enumerate(plan):
                rows = slice(i * GRID_W, (i + 1) * GRID_W)
                cols = slice(p_lo * 2 * GRID_W, (p_lo + len(entries)) * 2 * GRID_W)
                bias = jnp.concatenate([pt_ref[h, e] for e in entries], axis=1)
                s = s_ref[hh, rows, cols] + bias
                p = jnp.exp(s - jnp.max(s, axis=-1, keepdims=True))
                l_ref[hh, rows, :] = jnp.sum(p, axis=-1, keepdims=True)
                p_ref[hh, rows, cols] = p.astype(BF16)
        for hh in range(NA_HPS):
            lanes = slice(hh * HEAD_DIM, (hh + 1) * HEAD_DIM)
            v = jnp.concatenate([r[:, lanes] for r in (v0, v1, v2, v3)], axis=0)
            o = jnp.dot(p_ref[hh], v, preferred_element_type=F32)
            o_ref[:, lanes] = o / l_ref[hh]

    pl.when(g == 0)(functools.partial(attend, 0))
    pl.when((g > 0) & (g < last))(functools.partial(attend, NA_KH // 2))
    pl.when(g == last)(functools.partial(attend, NA_KH))


def _na_attention(proj, pt):
    s = proj.shape[0]
    rows = s // GRID_W
    assert rows % NA_G == 0 and rows >= 2 * NA_G
    tq, tkb = NA_G * GRID_W, NA_KBLK * GRID_W
    n_kb = rows // NA_KBLK
    width = NA_HPS * HEAD_DIM
    k_col, v_col = NA_HEADS // NA_HPS, 2 * NA_HEADS // NA_HPS

    def kv_spec(j, col0):
        def index(g, hp):
            first = jnp.clip(g * (NA_G // NA_KBLK) - 1, 0, n_kb - NA_NKB)
            return (first + j, col0 + hp)
        return pl.BlockSpec((tkb, width), index)

    return pl.pallas_call(
        functools.partial(_na_kernel, n_groups=rows // NA_G),
        grid=(rows // NA_G, NA_HEADS // NA_HPS),
        in_specs=([pl.BlockSpec((tq, width), lambda g, hp: (g, hp))]
                  + [kv_spec(j, k_col) for j in range(NA_NKB)]
                  + [kv_spec(j, v_col) for j in range(NA_NKB)]
                  + [pl.BlockSpec(pt.shape, lambda g, hp: (0, 0, 0, 0))]),
        out_specs=pl.BlockSpec((tq, width), lambda g, hp: (g, hp)),
        out_shape=jax.ShapeDtypeStruct((s, NA_WIDTH), F32),
        scratch_shapes=[pltpu.VMEM((NA_HPS, tq, NA_KR * GRID_W), F32),
                        pltpu.VMEM((NA_HPS, tq, NA_KR * GRID_W), BF16),
                        pltpu.VMEM((NA_HPS, tq, 1), F32)],
        compiler_params=_params(("parallel", "arbitrary"),
                                2 * pt.size * 4, NA_HPS * tq * NA_KR * GRID_W * 6,
                                NA_HPS * tq * V7X_LANES * 4, 2 * 3 * tq * width * 4),
        name="na_attention",
    )(proj, *([proj] * (2 * NA_NKB)), pt)


MLAPREP_TM = 512
V_ONES_ROWS = 16


def _mla_prep_kernel(qc_ref, kvc_ref, kpe_ref, gq_ref, gkv_ref, wqn_ref, wqp_ref, wqr_ref,
                     wkn_ref, wv_ref, cc_ref, ss_ref, cs_ref, q_ref, k_ref, v_ref):
    scale = MLA_QK_DIM ** -0.5 * LOG2_E
    qn = _rms(qc_ref[...].astype(F32), gq_ref[...]).astype(BF16)
    kvn = _rms(kvc_ref[...].astype(F32), gkv_ref[...]).astype(BF16)

    prod = kpe_ref[...].astype(F32) * cs_ref[...]
    k_pe = (prod + pltpu.roll(prod, QK_ROPE_DIM, 1))[:, :QK_ROPE_DIM].astype(BF16)

    cos = jnp.concatenate([cc_ref[...]] * (MLA_HEADS // 2), axis=1)
    sin = jnp.concatenate([ss_ref[...]] * (MLA_HEADS // 2), axis=1)
    q_pe = (jnp.dot(qn, wqp_ref[...], preferred_element_type=F32) * cos
            + jnp.dot(qn, wqr_ref[...], preferred_element_type=F32) * sin)
    q_pe_t = (q_pe * scale).T.astype(BF16)

    q_nope = jnp.dot(qn, wqn_ref[...], preferred_element_type=F32) * scale
    q_nope_t = q_nope.T.astype(BF16)
    k_nope = jnp.dot(kvn, wkn_ref[...], preferred_element_type=F32).astype(BF16)
    v_t = jnp.dot(kvn, wv_ref[...], preferred_element_type=F32).T.astype(BF16)
    for h in range(MLA_HEADS):
        nope = slice(h * QK_NOPE_DIM, (h + 1) * QK_NOPE_DIM)
        q_ref[h, :QK_NOPE_DIM, :] = q_nope_t[nope, :]
        q_ref[h, QK_NOPE_DIM:MLA_QK_DIM, :] = q_pe_t[h * QK_ROPE_DIM:(h + 1) * QK_ROPE_DIM, :]
        k_ref[h, :, :QK_NOPE_DIM] = k_nope[:, nope]
        k_ref[h, :, QK_NOPE_DIM:MLA_QK_DIM] = k_pe
        if q_ref.shape[1] > MLA_QK_DIM:
            q_ref[h, MLA_QK_DIM:, :] = jnp.zeros((q_ref.shape[1] - MLA_QK_DIM, q_ref.shape[2]), BF16)
            k_ref[h, :, MLA_QK_DIM:] = jnp.zeros((k_ref.shape[1], k_ref.shape[2] - MLA_QK_DIM), BF16)
        v_ref[h, 0, :V_HEAD_DIM, :] = v_t[h * V_HEAD_DIM:(h + 1) * V_HEAD_DIM, :]
        v_ref[h, 0, V_HEAD_DIM:, :] = jnp.ones((V_ONES_ROWS, v_t.shape[1]), BF16)


def _mla_prep(proj, g_q, g_kv, wqn, wqp, wqr, wkn, wv, cc, ss, cs, qk_width):
    s = proj.shape[0]
    tm = min(MLAPREP_TM, s)
    qc_blk = 3 * NA_WIDTH // Q_LORA_RANK
    kpe_blk = (3 * NA_WIDTH + Q_LORA_RANK + KV_LORA_RANK) // (2 * QK_ROPE_DIM)
    const = lambda i: (0, 0)
    return pl.pallas_call(
        _mla_prep_kernel,
        grid=(s // tm,),
        in_specs=[pl.BlockSpec((tm, Q_LORA_RANK), lambda i: (i, qc_blk)),
                  pl.BlockSpec((tm, KV_LORA_RANK), lambda i: (i, qc_blk + 1)),
                  pl.BlockSpec((tm, 2 * QK_ROPE_DIM), lambda i: (i, kpe_blk)),
                  pl.BlockSpec((1, Q_LORA_RANK), const), pl.BlockSpec((1, KV_LORA_RANK), const),
                  pl.BlockSpec(wqn.shape, const), pl.BlockSpec(wqp.shape, const),
                  pl.BlockSpec(wqr.shape, const), pl.BlockSpec(wkn.shape, const),
                  pl.BlockSpec(wv.shape, const),
                  pl.BlockSpec((tm, 2 * QK_ROPE_DIM), lambda i: (i, 0)),
                  pl.BlockSpec((tm, 2 * QK_ROPE_DIM), lambda i: (i, 0)),
                  pl.BlockSpec((tm, 2 * QK_ROPE_DIM), lambda i: (i, 0))],
        out_specs=[pl.BlockSpec((MLA_HEADS, qk_width, tm), lambda i: (0, 0, i)),
                   pl.BlockSpec((MLA_HEADS, tm, qk_width), lambda i: (0, i, 0)),
                   pl.BlockSpec((MLA_HEADS, 1, V_HEAD_DIM + V_ONES_ROWS, tm), lambda i: (0, i, 0, 0))],
        out_shape=[jax.ShapeDtypeStruct((MLA_HEADS, qk_width, s), BF16),
                   jax.ShapeDtypeStruct((MLA_HEADS, s, qk_width), BF16),
                   jax.ShapeDtypeStruct((MLA_HEADS, s // tm, V_HEAD_DIM + V_ONES_ROWS, tm), BF16)],
        compiler_params=_params(("parallel",), 2 * 2 * 512 * 4096, 2 * 3 * MLA_HEADS * tm * 256 * 2,
                                2 * tm * 1536 * 4, 8 * tm * 1024 * 4),
        name="mla_prep",
    )(proj, proj, proj, g_q, g_kv, wqn, wqp, wqr, wkn, wv, cc, ss, cs)


FLASH_TQ = 512


def _flash_kernel(qt_ref, k_ref, vt_ref, o_ref, acc_ref, s_ref, p_ref):
    tq = qt_ref.shape[2]
    n_slot, tk, _ = s_ref.shape
    tv = vt_ref.shape[3]
    n_sub = tk // tv
    n_k = vt_ref.shape[1] // n_sub
    assert n_k % n_slot == 0

    def scores(j):
        k = k_ref[0, pl.ds(pl.multiple_of(j * tk, tk), tk), :]
        return jnp.dot(k, qt_ref[0], preferred_element_type=F32)

    def accumulate(j, slot, alpha):
        pv = sum(jnp.dot(vt_ref[0, j * n_sub + u], p_ref[slot, u * tv:(u + 1) * tv, :],
                         preferred_element_type=F32) for u in range(n_sub))
        acc_ref[...] = alpha * acc_ref[...] + pv

    def step(j, slot, carry):
        m, alpha_prev, tile_max = carry
        s_next = scores(jnp.minimum(j + 1, n_k - 1))
        s_ref[(slot + 1) % n_slot] = s_next
        tile_max_next = jnp.max(s_next, axis=0, keepdims=True)
        accumulate(jnp.maximum(j - 1, 0), (slot - 1) % n_slot, alpha_prev)
        m_new = jnp.maximum(m, tile_max)
        alpha = jnp.exp2(m - m_new)
        p_ref[slot] = jnp.exp2((s_ref[slot] - m_new).astype(BF16))
        return m_new, alpha, tile_max_next

    def body(i, carry):
        for slot in range(n_slot):
            carry = step(n_slot * i + slot, slot, carry)
        return carry

    acc_ref[...] = jnp.zeros_like(acc_ref)
    p_ref[n_slot - 1] = jnp.zeros(p_ref.shape[1:], p_ref.dtype)
    s0 = scores(0)
    s_ref[0] = s0
    m0 = jnp.full((1, tq), -jnp.inf, F32)
    carry0 = (m0, jnp.ones((1, tq), F32), jnp.max(s0, axis=0, keepdims=True))
    _, alpha, _ = lax.fori_loop(0, n_k // n_slot, body, carry0)
    accumulate(n_k - 1, n_slot - 1, alpha)
    l = acc_ref[V_HEAD_DIM:V_HEAD_DIM + 1, :]
    o_ref[...] = (acc_ref[:V_HEAD_DIM, :] / l).T


def _flash_config(s):
    if s >= 16384:
        return 512, 512, 8
    return min(1024, s), min(512, s // 4), 4


def _flash(qt, k, vt, tq, tk, n_slot):
    _, s, qk_width = k.shape
    return pl.pallas_call(
        _flash_kernel,
        grid=(MLA_HEADS, s // tq),
        in_specs=[pl.BlockSpec((1, qk_width, tq), lambda h, i: (h, 0, i)),
                  pl.BlockSpec((1, s, qk_width), lambda h, i: (h, 0, 0)),
                  pl.BlockSpec((1,) + vt.shape[1:], lambda h, i: (h, 0, 0, 0))],
        out_specs=pl.BlockSpec((tq, V_HEAD_DIM), lambda h, i: (i, h)),
        out_shape=jax.ShapeDtypeStruct((s, MLA_WIDTH), F32),
        scratch_shapes=[pltpu.VMEM((V_HEAD_DIM + V_ONES_ROWS, tq), F32),
                        pltpu.VMEM((n_slot, tk, tq), F32),
                        pltpu.VMEM((n_slot, tk, tq), BF16)],
        compiler_params=_params(("parallel", "arbitrary"),
                                2 * s * 256 * 2, 2 * s * V_HEAD_DIM * 2, n_slot * tk * tq * 6),
        name="mla_flash",
    )(qt, k, vt)


OUTPROJ_TM = 256


def _outproj_kernel(na_ref, mla_ref, x_ref, gna_ref, gmla_ref, gt_ref, g_ref, sc_ref, sh_ref, w_ref,
                    o_ref, hn_ref):
    na = _rms(na_ref[...], gna_ref[...]).astype(BF16)
    mla = _rms(mla_ref[...], gmla_ref[...]).astype(BF16)
    y = (jnp.dot(na, w_ref[:NA_WIDTH, :], preferred_element_type=F32)
         + jnp.dot(mla, w_ref[NA_WIDTH:, :], preferred_element_type=F32))
    x1 = x_ref[...] + gt_ref[...] * y
    o_ref[...] = x1
    hn_ref[...] = (_rms(x1, g_ref[...]) * (1.0 + sc_ref[...]) + sh_ref[...]).astype(BF16)


def _outproj(o_na, o_mla, x, g_na, g_mla, gt, g_ffn, sc, sh, w):
    s, d = x.shape
    tm = min(OUTPROJ_TM, s)
    const = lambda i: (0, 0)
    vec = pl.BlockSpec((1, d), const)
    return pl.pallas_call(
        _outproj_kernel,
        grid=(s // tm,),
        in_specs=[pl.BlockSpec((tm, NA_WIDTH), lambda i: (i, 0)),
                  pl.BlockSpec((tm, MLA_WIDTH), lambda i: (i, 0)),
                  pl.BlockSpec((tm, d), lambda i: (i, 0)),
                  pl.BlockSpec((1, NA_WIDTH), const), pl.BlockSpec((1, MLA_WIDTH), const),
                  vec, vec, vec, vec, pl.BlockSpec(w.shape, const)],
        out_specs=[pl.BlockSpec((tm, d), lambda i: (i, 0)), pl.BlockSpec((tm, d), lambda i: (i, 0))],
        out_shape=[jax.ShapeDtypeStruct((s, d), F32), jax.ShapeDtypeStruct((s, d), BF16)],
        compiler_params=_params(("parallel",), 2 * w.size * 2, 2 * 3 * tm * d * 4, 2 * tm * d * 2),
        name="outproj",
    )(o_na, o_mla, x, g_na, g_mla, gt, g_ffn, sc, sh, w)


FFN_TM = 512
FFN_TF = 512


def _ffn_kernel(x_ref, hn_ref, gt_ref, gfin_ref, wg_ref, wu_ref, wd_ref, o_ref, acc_ref, *, n_f):
    j = pl.program_id(1)

    @pl.when(j == 0)
    def _():
        acc_ref[...] = jnp.zeros_like(acc_ref)

    h = hn_ref[...]
    a = jnp.dot(h, wg_ref[...], preferred_element_type=F32)
    b = jnp.dot(h, wu_ref[...], preferred_element_type=F32)
    t = (a * jax.nn.sigmoid(a) * b).astype(BF16)
    acc_ref[...] += jnp.dot(t, wd_ref[...], preferred_element_type=F32)

    @pl.when(j == n_f - 1)
    def _():
        o_ref[...] = _rms(x_ref[...] + gt_ref[...] * acc_ref[...], gfin_ref[...])


def _ffn(x, hn, gt, g_final, wg, wu, wd):
    s, d = x.shape
    tm, tf = min(FFN_TM, s), FFN_TF
    n_f = D_FF // tf
    const = lambda i, j: (0, 0)
    vec = pl.BlockSpec((1, d), const)
    row = pl.BlockSpec((tm, d), lambda i, j: (i, 0))
    return pl.pallas_call(
        functools.partial(_ffn_kernel, n_f=n_f),
        grid=(s // tm, n_f),
        in_specs=[row, row, vec, vec,
                  pl.BlockSpec((d, tf), lambda i, j: (0, j)),
                  pl.BlockSpec((d, tf), lambda i, j: (0, j)),
                  pl.BlockSpec((tf, d), lambda i, j: (j, 0))],
        out_specs=row,
        out_shape=jax.ShapeDtypeStruct((s, d), F32),
        scratch_shapes=[pltpu.VMEM((tm, d), F32)],
        compiler_params=_params(("parallel", "arbitrary"),
                                4 * tm * d * 4, 2 * tm * d * 2, tm * d * 4, 2 * 3 * d * tf * 2,
                                3 * tm * tf * 4),
        name="ffn",
    )(x, hn, gt, g_final, wg, wu, wd)


def _rotate_half_cols(w):
    half = QK_ROPE_DIM // 2
    return jnp.concatenate([-w[..., half:], w[..., :half]], axis=-1)


def _rope_tables(s):
    pos = jnp.arange(s, dtype=F32)
    inv_freq = ROPE_THETA ** (-jnp.arange(0, QK_ROPE_DIM, 2, dtype=F32) / QK_ROPE_DIM)
    ang = pos[:, None] * inv_freq[None, :]
    cos, sin = jnp.cos(ang), jnp.sin(ang)
    return (jnp.concatenate([cos] * 4, axis=-1), jnp.concatenate([sin] * 4, axis=-1),
            jnp.concatenate([cos, cos, sin, sin], axis=-1))


def _layer_weights(l, w_in, w_uq, w_ukv, w_o, w_gate, w_up, w_down):
    k_pe0 = 3 * NA_WIDTH + Q_LORA_RANK + KV_LORA_RANK
    w_in_l = w_in[l]
    w_in_ext = jnp.concatenate([w_in_l, _rotate_half_cols(w_in_l[:, k_pe0:])], axis=1).astype(BF16)
    uq = w_uq[l].reshape(Q_LORA_RANK, MLA_HEADS, MLA_QK_DIM)
    wqn = uq[:, :, :QK_NOPE_DIM].reshape(Q_LORA_RANK, -1).astype(BF16)
    wqp = uq[:, :, QK_NOPE_DIM:]
    wqr = _rotate_half_cols(wqp).reshape(Q_LORA_RANK, -1).astype(BF16)
    wqp = wqp.reshape(Q_LORA_RANK, -1).astype(BF16)
    ukv = w_ukv[l].reshape(KV_LORA_RANK, MLA_HEADS, QK_NOPE_DIM + V_HEAD_DIM)
    wkn = ukv[:, :, :QK_NOPE_DIM].reshape(KV_LORA_RANK, -1).astype(BF16)
    wv = ukv[:, :, QK_NOPE_DIM:].reshape(KV_LORA_RANK, -1).astype(BF16)
    return dict(w_in=w_in_ext, wqn=wqn, wqp=wqp, wqr=wqr, wkn=wkn, wv=wv,
                w_o=w_o[l].astype(BF16), w_gate=w_gate[l].astype(BF16),
                w_up=w_up[l].astype(BF16), w_down=w_down[l].astype(BF16))


def _encoder_layer(x, mod, lw, pt, tables, g_attn, g_q, g_kv, g_out_na, g_out_mla, g_ffn, g_final):
    d = D_MODEL
    sh1, sc1, gt1, sh2, sc2, gt2 = [mod[:, i * d:(i + 1) * d] for i in range(6)]
    proj = _inproj(x, g_attn, sc1, sh1, lw["w_in"])
    o_na = _na_attention(proj, pt)
    q, k, v = _mla_prep(proj, g_q, g_kv, lw["wqn"], lw["wqp"], lw["wqr"], lw["wkn"], lw["wv"], *tables,
                        qk_width=MLA_QK_DIM)
    o_mla = _flash(q, k, v, *_flash_config(x.shape[0]))
    x, hn = _outproj(o_na, o_mla, x, g_out_na, g_out_mla, gt1, g_ffn, sc2, sh2, lw["w_o"])
    return _ffn(x, hn, gt2, g_final, lw["w_gate"], lw["w_up"], lw["w_down"])


def kernel(x_prompt, x_sample, c_prompt, c_sample, w_ada, b_ada, g_attn, w_in, rpb, g_q, w_uq,
           g_kv, w_ukv, g_out_na, g_out_mla, w_o, g_ffn, w_gate, w_up, w_down, g_final):
    assert DEPTH == 1 and w_ada.shape[0] == DEPTH
    assert x_prompt.shape[0] == 1 and x_sample.shape[0] == 1
    l = 0
    c_cols = jnp.concatenate([c_prompt, c_sample], axis=0).T
    mod = _adaln(c_cols, w_ada[l], b_ada[l][None, :])
    lw = _layer_weights(l, w_in, w_uq, w_ukv, w_o, w_gate, w_up, w_down)
    pt = _na_bias(rpb[l])
    row = lambda a: a[None, :]
    tables = _rope_tables(max(x_prompt.shape[1], x_sample.shape[1]))
    outs = []
    for b, x in enumerate((x_prompt, x_sample)):
        y = _encoder_layer(x[0], mod[b:b + 1], lw, pt, tables,
                           row(g_attn[l]), row(g_q[l]), row(g_kv[l]), row(g_out_na[l]),
                           row(g_out_mla[l]), row(g_ffn[l]), row(g_final))
        outs.append(y[None])
    return tuple(outs)
```

```python
import functools

import jax
import jax.numpy as jnp
from jax import lax
from jax.experimental import pallas as pl
from jax.experimental.pallas import tpu as pltpu

D_MODEL = 2048
DEPTH = 1
GRID_W = 64
HEAD_DIM = 128
NA_HEADS = 8
NA_WIDTH = NA_HEADS * HEAD_DIM
NA_KH = 8
NA_KW = 16
MLA_HEADS = 8
Q_LORA_RANK = 512
KV_LORA_RANK = 512
QK_NOPE_DIM = 128
QK_ROPE_DIM = 64
V_HEAD_DIM = 128
MLA_QK_DIM = QK_NOPE_DIM + QK_ROPE_DIM
MLA_WIDTH = MLA_HEADS * V_HEAD_DIM
D_FF = 5632
ROPE_THETA = 10000.0
RMS_EPS = 1e-6

PROJ_WIDTH = 3 * NA_WIDTH + Q_LORA_RANK + KV_LORA_RANK + 2 * QK_ROPE_DIM

V7X_VMEM_BYTES = 64 * 1024 * 1024
V7X_LANES = 128
NEG = -1e30
LOG2_E = 1.4426950408889634

BF16 = jnp.bfloat16
F32 = jnp.float32
_NT = (((1,), (1,)), ((), ()))


def _vmem_limit(*buffer_bytes):
    need = int(sum(buffer_bytes) * 1.5) + (4 << 20)
    return min(max(need, 32 << 20), V7X_VMEM_BYTES - (8 << 20))


def _params(semantics, *buffer_bytes):
    return pltpu.CompilerParams(dimension_semantics=semantics,
                                vmem_limit_bytes=_vmem_limit(*buffer_bytes))


def _rms(x, g):
    return x * lax.rsqrt(jnp.mean(x * x, axis=-1, keepdims=True) + RMS_EPS) * g


ADA_TN = 512


def _adaln_kernel(c_ref, w_ref, b_ref, o_ref):
    c = c_ref[...]
    cs = c * jax.nn.sigmoid(c)
    w = w_ref[...]
    r0 = jnp.sum(w * cs[:, 0:1], axis=0, keepdims=True)
    r1 = jnp.sum(w * cs[:, 1:2], axis=0, keepdims=True)
    o_ref[...] = jnp.concatenate([r0, r1], axis=0) + b_ref[...]


def _adaln(c_cols, w, b):
    d, n = w.shape
    return pl.pallas_call(
        _adaln_kernel,
        grid=(n // ADA_TN,),
        in_specs=[pl.BlockSpec((d, 2), lambda j: (0, 0)),
                  pl.BlockSpec((d, ADA_TN), lambda j: (0, j)),
                  pl.BlockSpec((1, ADA_TN), lambda j: (0, j))],
        out_specs=pl.BlockSpec((2, ADA_TN), lambda j: (0, j)),
        out_shape=jax.ShapeDtypeStruct((2, n), F32),
        compiler_params=_params(("arbitrary",), 2 * d * ADA_TN * 4, 2 * d * V7X_LANES * 4),
        name="adaln",
    )(c_cols, w, b)


INPROJ_TM = 1024
INPROJ_TN = PROJ_WIDTH // 3


def _inproj_kernel(x_ref, g_ref, sc_ref, sh_ref, w_ref, o_ref, hn_ref, *, n_blocks):
    r, j = pl.program_id(0), pl.program_id(1)

    def normalise(slot):
        h = _rms(x_ref[...], g_ref[...]) * (1.0 + sc_ref[...]) + sh_ref[...]
        hn_ref[slot] = h.astype(BF16)

    def project(slot):
        o_ref[...] = jnp.dot(hn_ref[slot], w_ref[...], preferred_element_type=F32).astype(o_ref.dtype)

    pl.when((r == 0) & (j == 0))(functools.partial(normalise, 0))
    for parity in (0, 1):
        mine = (r > 0) & (r % 2 == parity)
        read, write = 1 - parity, parity

        @pl.when(mine & (j == 0) & (r < n_blocks))
        def _():
            project(read)
            normalise(write)

        @pl.when(mine & ((j > 0) | (r == n_blocks)))
        def _():
            project(read)


def _inproj(x, g, sc, sh, w):
    s, d = x.shape
    tm, tn = min(INPROJ_TM, s), INPROJ_TN
    n_blocks = s // tm
    row = lambda r, j: (0, 0)
    return pl.pallas_call(
        functools.partial(_inproj_kernel, n_blocks=n_blocks),
        grid=(n_blocks + 1, PROJ_WIDTH // tn),
        in_specs=[pl.BlockSpec((tm, d), lambda r, j: (jnp.minimum(r, n_blocks - 1), 0)),
                  pl.BlockSpec((1, d), row), pl.BlockSpec((1, d), row), pl.BlockSpec((1, d), row),
                  pl.BlockSpec((d, tn), lambda r, j: (0, j))],
        out_specs=pl.BlockSpec((tm, tn), lambda r, j: (jnp.maximum(r - 1, 0), jnp.where(r == 0, 0, j))),
        out_shape=jax.ShapeDtypeStruct((s, PROJ_WIDTH), BF16),
        scratch_shapes=[pltpu.VMEM((2, tm, d), BF16)],
        compiler_params=_params(("arbitrary", "arbitrary"),
                                2 * tm * d * 4, 2 * tm * d * 2, 2 * d * tn * 2, 2 * tm * tn * 2),
        name="inproj",
    )(x, g, sc, sh, w)


NA_G = 8
NA_KR = NA_G + NA_KH
NA_KBLK = 4
NA_NKB = NA_KR // NA_KBLK
NA_HPS = 4
NA_PT_LEFT_MASKED = 14
NA_PT_RIGHT_MASKED = 15
NA_PT_ENTRIES = 16
RPB_ROWS = 2 * NA_KH - 1
RPB_COLS = 2 * NA_KW - 1


def _na_bias_kernel(rpb_ref, o_ref):
    h = pl.program_id(0)
    qc = lax.broadcasted_iota(jnp.int32, (GRID_W, 2 * GRID_W), 0)
    kk = lax.broadcasted_iota(jnp.int32, (GRID_W, 2 * GRID_W), 1)
    right = kk >= GRID_W
    kc = jnp.where(right, kk - GRID_W, kk)
    dc = kc - qc + (NA_KW - 1)
    c0 = jnp.clip(qc - NA_KW // 2, 0, GRID_W - NA_KW)
    col_ok = (kc >= c0) & (kc < c0 + NA_KW)

    def tile(dr_left, dr_right):
        acc = jnp.full((GRID_W, 2 * GRID_W), NEG, F32)
        for d in range(RPB_COLS):
            base = h * (RPB_ROWS * RPB_COLS) + d
            left = NEG if dr_left is None else rpb_ref[base + dr_left * RPB_COLS]
            rght = NEG if dr_right is None else rpb_ref[base + dr_right * RPB_COLS]
            acc = jnp.where(dc == d, jnp.where(right, rght, left), acc)
        return jnp.where(col_ok, acc, NEG)

    for e in range(NA_PT_LEFT_MASKED):
        o_ref[0, e] = tile(e, e + 1)
    o_ref[0, NA_PT_LEFT_MASKED] = tile(None, 3)
    o_ref[0, NA_PT_RIGHT_MASKED] = tile(10, None)


def _na_bias(rpb):
    return pl.pallas_call(
        _na_bias_kernel,
        grid=(NA_HEADS,),
        in_specs=[pl.BlockSpec(memory_space=pltpu.SMEM)],
        out_specs=pl.BlockSpec((1, NA_PT_ENTRIES, GRID_W, 2 * GRID_W), lambda h: (h, 0, 0, 0)),
        out_shape=jax.ShapeDtypeStruct((NA_HEADS, NA_PT_ENTRIES, GRID_W, 2 * GRID_W), F32),
        compiler_params=_params(("arbitrary",)),
        name="na_bias",
    )(rpb.reshape(-1))


def _na_row_plan(q_off):
    plan = []
    for i in range(NA_G):
        qr = q_off + i
        rs = min(max(qr - NA_KH // 2, 0), NA_KR - NA_KH)
        p_lo, p_hi = rs // 2, (rs + NA_KH - 1) // 2
        entries = []
        for p in range(p_lo, p_hi + 1):
            ok_l = rs <= 2 * p < rs + NA_KH
            ok_r = rs <= 2 * p + 1 < rs + NA_KH
            dr0 = 2 * p - qr + (NA_KH - 1)
            if ok_l and ok_r:
                assert 0 <= dr0 < NA_PT_LEFT_MASKED
                entries.append(dr0)
            elif ok_r:
                assert dr0 + 1 == 3
                entries.append(NA_PT_LEFT_MASKED)
            else:
                assert ok_l and dr0 == 10
                entries.append(NA_PT_RIGHT_MASKED)
        plan.append((p_lo, tuple(entries)))
    return plan


def _na_kernel(q_ref, k0, k1, k2, k3, v0, v1, v2, v3, pt_ref, o_ref, s_ref, p_ref, l_ref, *, n_groups):
    g, hp = pl.program_id(0), pl.program_id(1)
    last = n_groups - 1
    scale = HEAD_DIM ** -0.5

    def attend(q_off):
        plan = _na_row_plan(q_off)
        for hh in range(NA_HPS):
            lanes = slice(hh * HEAD_DIM, (hh + 1) * HEAD_DIM)
            q = (q_ref[:, lanes].astype(F32) * scale).astype(BF16)
            k = jnp.concatenate([r[:, lanes] for r in (k0, k1, k2, k3)], axis=0)
            s_ref[hh] = lax.dot_general(q, k, _NT, preferred_element_type=F32)
            p_ref[hh] = jnp.zeros(p_ref.shape[1:], p_ref.dtype)
        for hh in range(NA_HPS):
            h = hp * NA_HPS + hh
            for i, (p_lo, entries) in enumerate(plan):
                rows = slice(i * GRID_W, (i + 1) * GRID_W)
                cols = slice(p_lo * 2 * GRID_W, (p_lo + len(entries)) * 2 * GRID_W)
                bias = jnp.concatenate([pt_ref[h, e] for e in entries], axis=1)
                s = s_ref[hh, rows, cols] + bias
                p = jnp.exp(s - jnp.max(s, axis=-1, keepdims=True))
                l_ref[hh, rows, :] = jnp.sum(p, axis=-1, keepdims=True)
                p_ref[hh, rows, cols] = p.astype(BF16)
        for hh in range(NA_HPS):
            lanes = slice(hh * HEAD_DIM, (hh + 1) * HEAD_DIM)
            v = jnp.concatenate([r[:, lanes] for r in (v0, v1, v2, v3)], axis=0)
            o = jnp.dot(p_ref[hh], v, preferred_element_type=F32)
            o_ref[:, lanes] = o / l_ref[hh]

    pl.when(g == 0)(functools.partial(attend, 0))
    pl.when((g > 0) & (g < last))(functools.partial(attend, NA_KH // 2))
    pl.when(g == last)(functools.partial(attend, NA_KH))


def _na_attention(proj, pt):
    s = proj.shape[0]
    rows = s // GRID_W
    assert rows % NA_G == 0 and rows >= 2 * NA_G
    tq, tkb = NA_G * GRID_W, NA_KBLK * GRID_W
    n_kb = rows // NA_KBLK
    width = NA_HPS * HEAD_DIM
    k_col, v_col = NA_HEADS // NA_HPS, 2 * NA_HEADS // NA_HPS

    def kv_spec(j, col0):
        def index(g, hp):
            first = jnp.clip(g * (NA_G // NA_KBLK) - 1, 0, n_kb - NA_NKB)
            return (first + j, col0 + hp)
        return pl.BlockSpec((tkb, width), index)

    return pl.pallas_call(
        functools.partial(_na_kernel, n_groups=rows // NA_G),
        grid=(rows // NA_G, NA_HEADS // NA_HPS),
        in_specs=([pl.BlockSpec((tq, width), lambda g, hp: (g, hp))]
                  + [kv_spec(j, k_col) for j in range(NA_NKB)]
                  + [kv_spec(j, v_col) for j in range(NA_NKB)]
                  + [pl.BlockSpec(pt.shape, lambda g, hp: (0, 0, 0, 0))]),
        out_specs=pl.BlockSpec((tq, width), lambda g, hp: (g, hp)),
        out_shape=jax.ShapeDtypeStruct((s, NA_WIDTH), F32),
        scratch_shapes=[pltpu.VMEM((NA_HPS, tq, NA_KR * GRID_W), F32),
                        pltpu.VMEM((NA_HPS, tq, NA_KR * GRID_W), BF16),
                        pltpu.VMEM((NA_HPS, tq, 1), F32)],
        compiler_params=_params(("parallel", "arbitrary"),
                                2 * pt.size * 4, NA_HPS * tq * NA_KR * GRID_W * 6,
                                NA_HPS * tq * V7X_LANES * 4, 2 * 3 * tq * width * 4),
        name="na_attention",
    )(proj, *([proj] * (2 * NA_NKB)), pt)


MLAPREP_TM = 512
V_ONES_ROWS = 16


def _mla_prep_kernel(qc_ref, kvc_ref, kpe_ref, gq_ref, gkv_ref, wqn_ref, wqp_ref, wqr_ref,
                     wkn_ref, wv_ref, cc_ref, ss_ref, cs_ref, q_ref, k_ref, v_ref):
    scale = MLA_QK_DIM ** -0.5 * LOG2_E
    qn = _rms(qc_ref[...].astype(F32), gq_ref[...]).astype(BF16)
    kvn = _rms(kvc_ref[...].astype(F32), gkv_ref[...]).astype(BF16)

    prod = kpe_ref[...].astype(F32) * cs_ref[...]
    k_pe = (prod + pltpu.roll(prod, QK_ROPE_DIM, 1))[:, :QK_ROPE_DIM].astype(BF16)

    cos = jnp.concatenate([cc_ref[...]] * (MLA_HEADS // 2), axis=1)
    sin = jnp.concatenate([ss_ref[...]] * (MLA_HEADS // 2), axis=1)
    q_pe = (jnp.dot(qn, wqp_ref[...], preferred_element_type=F32) * cos
            + jnp.dot(qn, wqr_ref[...], preferred_element_type=F32) * sin)
    q_pe_t = (q_pe * scale).T.astype(BF16)

    q_nope = jnp.dot(qn, wqn_ref[...], preferred_element_type=F32) * scale
    q_nope_t = q_nope.T.astype(BF16)
    k_nope = jnp.dot(kvn, wkn_ref[...], preferred_element_type=F32).astype(BF16)
    v_t = jnp.dot(kvn, wv_ref[...], preferred_element_type=F32).T.astype(BF16)
    for h in range(MLA_HEADS):
        nope = slice(h * QK_NOPE_DIM, (h + 1) * QK_NOPE_DIM)
        q_ref[h, :QK_NOPE_DIM, :] = q_nope_t[nope, :]
        q_ref[h, QK_NOPE_DIM:, :] = q_pe_t[h * QK_ROPE_DIM:(h + 1) * QK_ROPE_DIM, :]
        k_ref[h, :, :QK_NOPE_DIM] = k_nope[:, nope]
        k_ref[h, :, QK_NOPE_DIM:] = k_pe
        v_ref[h, 0, :V_HEAD_DIM, :] = v_t[h * V_HEAD_DIM:(h + 1) * V_HEAD_DIM, :]
        v_ref[h, 0, V_HEAD_DIM:, :] = jnp.ones((V_ONES_ROWS, v_t.shape[1]), BF16)


def _mla_prep(proj, g_q, g_kv, wqn, wqp, wqr, wkn, wv, cc, ss, cs):
    s = proj.shape[0]
    qk_width = MLA_QK_DIM
    tm = min(MLAPREP_TM, s)
    qc_blk = 3 * NA_WIDTH // Q_LORA_RANK
    kpe_blk = (3 * NA_WIDTH + Q_LORA_RANK + KV_LORA_RANK) // (2 * QK_ROPE_DIM)
    const = lambda i: (0, 0)
    return pl.pallas_call(
        _mla_prep_kernel,
        grid=(s // tm,),
        in_specs=[pl.BlockSpec((tm, Q_LORA_RANK), lambda i: (i, qc_blk)),
                  pl.BlockSpec((tm, KV_LORA_RANK), lambda i: (i, qc_blk + 1)),
                  pl.BlockSpec((tm, 2 * QK_ROPE_DIM), lambda i: (i, kpe_blk)),
                  pl.BlockSpec((1, Q_LORA_RANK), const), pl.BlockSpec((1, KV_LORA_RANK), const),
                  pl.BlockSpec(wqn.shape, const), pl.BlockSpec(wqp.shape, const),
                  pl.BlockSpec(wqr.shape, const), pl.BlockSpec(wkn.shape, const),
                  pl.BlockSpec(wv.shape, const),
                  pl.BlockSpec((tm, 2 * QK_ROPE_DIM), lambda i: (i, 0)),
                  pl.BlockSpec((tm, 2 * QK_ROPE_DIM), lambda i: (i, 0)),
                  pl.BlockSpec((tm, 2 * QK_ROPE_DIM), lambda i: (i, 0))],
        out_specs=[pl.BlockSpec((MLA_HEADS, qk_width, tm), lambda i: (0, 0, i)),
                   pl.BlockSpec((MLA_HEADS, tm, qk_width), lambda i: (0, i, 0)),
                   pl.BlockSpec((MLA_HEADS, 1, V_HEAD_DIM + V_ONES_ROWS, tm), lambda i: (0, i, 0, 0))],
        out_shape=[jax.ShapeDtypeStruct((MLA_HEADS, qk_width, s), BF16),
                   jax.ShapeDtypeStruct((MLA_HEADS, s, qk_width), BF16),
                   jax.ShapeDtypeStruct((MLA_HEADS, s // tm, V_HEAD_DIM + V_ONES_ROWS, tm), BF16)],
        compiler_params=_params(("parallel",), 2 * 2 * 512 * 4096, 2 * 3 * MLA_HEADS * tm * 256 * 2,
                                2 * tm * 1536 * 4, 8 * tm * 1024 * 4),
        name="mla_prep",
    )(proj, proj, proj, g_q, g_kv, wqn, wqp, wqr, wkn, wv, cc, ss, cs)


FLASH_TQ = 512
FLASH_SLOTS = 8


def _flash_kernel(qt_ref, k_ref, vt_ref, o_ref, acc_ref, s_ref, p_ref):
    tq = qt_ref.shape[2]
    n_slot, tk, _ = s_ref.shape
    tv = vt_ref.shape[3]
    n_sub = tk // tv
    n_k = vt_ref.shape[1] // n_sub
    assert n_k % n_slot == 0

    def scores(j):
        k = k_ref[0, pl.ds(pl.multiple_of(j * tk, tk), tk), :]
        return jnp.dot(k, qt_ref[0], preferred_element_type=F32)

    def accumulate(j, slot, alpha):
        pv = sum(jnp.dot(vt_ref[0, j * n_sub + u], p_ref[slot, u * tv:(u + 1) * tv, :],
                         preferred_element_type=F32) for u in range(n_sub))
        acc_ref[...] = alpha * acc_ref[...] + pv

    def step(j, slot, carry):
        m, alpha_prev, tile_max = carry
        s_next = scores(jnp.minimum(j + 1, n_k - 1))
        s_ref[(slot + 1) % n_slot] = s_next
        tile_max_next = jnp.max(s_next, axis=0, keepdims=True)
        accumulate(jnp.maximum(j - 1, 0), (slot - 1) % n_slot, alpha_prev)
        m_new = jnp.maximum(m, tile_max)
        alpha = jnp.exp2(m - m_new)
        p_ref[slot] = jnp.exp2((s_ref[slot] - m_new).astype(BF16))
        return m_new, alpha, tile_max_next

    def body(i, carry):
        for slot in range(n_slot):
            carry = step(n_slot * i + slot, slot, carry)
        return carry

    acc_ref[...] = jnp.zeros_like(acc_ref)
    p_ref[n_slot - 1] = jnp.zeros(p_ref.shape[1:], p_ref.dtype)
    s0 = scores(0)
    s_ref[0] = s0
    m0 = jnp.full((1, tq), -jnp.inf, F32)
    carry0 = (m0, jnp.ones((1, tq), F32), jnp.max(s0, axis=0, keepdims=True))
    _, alpha, _ = lax.fori_loop(0, n_k // n_slot, body, carry0)
    accumulate(n_k - 1, n_slot - 1, alpha)
    l = acc_ref[V_HEAD_DIM:V_HEAD_DIM + 1, :]
    o_ref[...] = (acc_ref[:V_HEAD_DIM, :] / l).T


def _flash(qt, k, vt):
    _, s, qk_width = k.shape
    tq, tk = min(FLASH_TQ, s), vt.shape[3]
    n_slot = min(FLASH_SLOTS, s // tk)
    return pl.pallas_call(
        _flash_kernel,
        grid=(MLA_HEADS, s // tq),
        in_specs=[pl.BlockSpec((1, qk_width, tq), lambda h, i: (h, 0, i)),
                  pl.BlockSpec((1, s, qk_width), lambda h, i: (h, 0, 0)),
                  pl.BlockSpec((1,) + vt.shape[1:], lambda h, i: (h, 0, 0, 0))],
        out_specs=pl.BlockSpec((tq, V_HEAD_DIM), lambda h, i: (i, h)),
        out_shape=jax.ShapeDtypeStruct((s, MLA_WIDTH), F32),
        scratch_shapes=[pltpu.VMEM((V_HEAD_DIM + V_ONES_ROWS, tq), F32),
                        pltpu.VMEM((n_slot, tk, tq), F32),
                        pltpu.VMEM((n_slot, tk, tq), BF16)],
        compiler_params=_params(("parallel", "arbitrary"),
                                2 * s * 256 * 2, 2 * s * V_HEAD_DIM * 2, n_slot * tk * tq * 6),
        name="mla_flash",
    )(qt, k, vt)


OUTPROJ_TM = 512


def _outproj_kernel(na_ref, mla_ref, x_ref, gna_ref, gmla_ref, gt_ref, g_ref, sc_ref, sh_ref, w_ref,
                    o_ref, hn_ref):
    na = _rms(na_ref[...], gna_ref[...]).astype(BF16)
    mla = _rms(mla_ref[...], gmla_ref[...]).astype(BF16)
    y = (jnp.dot(na, w_ref[:NA_WIDTH, :], preferred_element_type=F32)
         + jnp.dot(mla, w_ref[NA_WIDTH:, :], preferred_element_type=F32))
    x1 = x_ref[...] + gt_ref[...] * y
    o_ref[...] = x1
    hn_ref[...] = (_rms(x1, g_ref[...]) * (1.0 + sc_ref[...]) + sh_ref[...]).astype(BF16)


def _outproj(o_na, o_mla, x, g_na, g_mla, gt, g_ffn, sc, sh, w):
    s, d = x.shape
    tm = min(OUTPROJ_TM, s)
    const = lambda i: (0, 0)
    vec = pl.BlockSpec((1, d), const)
    return pl.pallas_call(
        _outproj_kernel,
        grid=(s // tm,),
        in_specs=[pl.BlockSpec((tm, NA_WIDTH), lambda i: (i, 0)),
                  pl.BlockSpec((tm, MLA_WIDTH), lambda i: (i, 0)),
                  pl.BlockSpec((tm, d), lambda i: (i, 0)),
                  pl.BlockSpec((1, NA_WIDTH), const), pl.BlockSpec((1, MLA_WIDTH), const),
                  vec, vec, vec, vec, pl.BlockSpec(w.shape, const)],
        out_specs=[pl.BlockSpec((tm, d), lambda i: (i, 0)), pl.BlockSpec((tm, d), lambda i: (i, 0))],
        out_shape=[jax.ShapeDtypeStruct((s, d), F32), jax.ShapeDtypeStruct((s, d), BF16)],
        compiler_params=_params(("parallel",), 2 * w.size * 2, 2 * 3 * tm * d * 4, 2 * tm * d * 2),
        name="outproj",
    )(o_na, o_mla, x, g_na, g_mla, gt, g_ffn, sc, sh, w)


FFN_TM = 512
FFN_TF = 512


def _ffn_kernel(x_ref, hn_ref, gt_ref, gfin_ref, wg_ref, wu_ref, wd_ref, o_ref, acc_ref, *, n_f):
    j = pl.program_id(1)

    @pl.when(j == 0)
    def _():
        acc_ref[...] = jnp.zeros_like(acc_ref)

    h = hn_ref[...]
    a = jnp.dot(h, wg_ref[...], preferred_element_type=F32)
    b = jnp.dot(h, wu_ref[...], preferred_element_type=F32)
    t = (a * jax.nn.sigmoid(a) * b).astype(BF16)
    acc_ref[...] += jnp.dot(t, wd_ref[...], preferred_element_type=F32)

    @pl.when(j == n_f - 1)
    def _():
        o_ref[...] = _rms(x_ref[...] + gt_ref[...] * acc_ref[...], gfin_ref[...])


def _ffn(x, hn, gt, g_final, wg, wu, wd):
    s, d = x.shape
    tm, tf = min(FFN_TM, s), FFN_TF
    n_f = D_FF // tf
    const = lambda i, j: (0, 0)
    vec = pl.BlockSpec((1, d), const)
    row = pl.BlockSpec((tm, d), lambda i, j: (i, 0))
    return pl.pallas_call(
        functools.partial(_ffn_kernel, n_f=n_f),
        grid=(s // tm, n_f),
        in_specs=[row, row, vec, vec,
                  pl.BlockSpec((d, tf), lambda i, j: (0, j)),
                  pl.BlockSpec((d, tf), lambda i, j: (0, j)),
                  pl.BlockSpec((tf, d), lambda i, j: (j, 0))],
        out_specs=row,
        out_shape=jax.ShapeDtypeStruct((s, d), F32),
        scratch_shapes=[pltpu.VMEM((tm, d), F32)],
        compiler_params=_params(("parallel", "arbitrary"),
                                4 * tm * d * 4, 2 * tm * d * 2, tm * d * 4, 2 * 3 * d * tf * 2,
                                3 * tm * tf * 4),
        name="ffn",
    )(x, hn, gt, g_final, wg, wu, wd)


def _rotate_half_cols(w):
    half = QK_ROPE_DIM // 2
    return jnp.concatenate([-w[..., half:], w[..., :half]], axis=-1)


def _rope_tables(s):
    pos = jnp.arange(s, dtype=F32)
    inv_freq = ROPE_THETA ** (-jnp.arange(0, QK_ROPE_DIM, 2, dtype=F32) / QK_ROPE_DIM)
    ang = pos[:, None] * inv_freq[None, :]
    cos, sin = jnp.cos(ang), jnp.sin(ang)
    return (jnp.concatenate([cos] * 4, axis=-1), jnp.concatenate([sin] * 4, axis=-1),
            jnp.concatenate([cos, cos, sin, sin], axis=-1))


def _layer_weights(l, w_in, w_uq, w_ukv, w_o, w_gate, w_up, w_down):
    k_pe0 = 3 * NA_WIDTH + Q_LORA_RANK + KV_LORA_RANK
    w_in_l = w_in[l]
    w_in_ext = jnp.concatenate([w_in_l, _rotate_half_cols(w_in_l[:, k_pe0:])], axis=1).astype(BF16)
    uq = w_uq[l].reshape(Q_LORA_RANK, MLA_HEADS, MLA_QK_DIM)
    wqn = uq[:, :, :QK_NOPE_DIM].reshape(Q_LORA_RANK, -1).astype(BF16)
    wqp = uq[:, :, QK_NOPE_DIM:]
    wqr = _rotate_half_cols(wqp).reshape(Q_LORA_RANK, -1).astype(BF16)
    wqp = wqp.reshape(Q_LORA_RANK, -1).astype(BF16)
    ukv = w_ukv[l].reshape(KV_LORA_RANK, MLA_HEADS, QK_NOPE_DIM + V_HEAD_DIM)
    wkn = ukv[:, :, :QK_NOPE_DIM].reshape(KV_LORA_RANK, -1).astype(BF16)
    wv = ukv[:, :, QK_NOPE_DIM:].reshape(KV_LORA_RANK, -1).astype(BF16)
    return dict(w_in=w_in_ext, wqn=wqn, wqp=wqp, wqr=wqr, wkn=wkn, wv=wv,
                w_o=w_o[l].astype(BF16), w_gate=w_gate[l].astype(BF16),
                w_up=w_up[l].astype(BF16), w_down=w_down[l].astype(BF16))


def _encoder_layer(x, mod, lw, pt, tables, g_attn, g_q, g_kv, g_out_na, g_out_mla, g_ffn, g_final):
    d = D_MODEL
    sh1, sc1, gt1, sh2, sc2, gt2 = [mod[:, i * d:(i + 1) * d] for i in range(6)]
    proj = _inproj(x, g_attn, sc1, sh1, lw["w_in"])
    o_na = _na_attention(proj, pt)
    q, k, v = _mla_prep(proj, g_q, g_kv, lw["wqn"], lw["wqp"], lw["wqr"], lw["wkn"], lw["wv"], *tables)
    o_mla = _flash(q, k, v)
    x, hn = _outproj(o_na, o_mla, x, g_out_na, g_out_mla, gt1, g_ffn, sc2, sh2, lw["w_o"])
    return _ffn(x, hn, gt2, g_final, lw["w_gate"], lw["w_up"], lw["w_down"])


def kernel(x_prompt, x_sample, c_prompt, c_sample, w_ada, b_ada, g_attn, w_in, rpb, g_q, w_uq,
           g_kv, w_ukv, g_out_na, g_out_mla, w_o, g_ffn, w_gate, w_up, w_down, g_final):
    assert DEPTH == 1 and w_ada.shape[0] == DEPTH
    assert x_prompt.shape[0] == 1 and x_sample.shape[0] == 1
    l = 0
    c_cols = jnp.concatenate([c_prompt, c_sample], axis=0).T
    mod = _adaln(c_cols, w_ada[l], b_ada[l][None, :])
    lw = _layer_weights(l, w_in, w_uq, w_ukv, w_o, w_gate, w_up, w_down)
    pt = _na_bias(rpb[l])
    row = lambda a: a[None, :]
    tables = _rope_tables(max(x_prompt.shape[1], x_sample.shape[1]))
    outs = []
    for b, x in enumerate((x_prompt, x_sample)):
        y = _encoder_layer(x[0], mod[b:b + 1], lw, pt, tables,
                           row(g_attn[l]), row(g_q[l]), row(g_kv[l]), row(g_out_na[l]),
                           row(g_out_mla[l]), row(g_ffn[l]), row(g_final))
        outs.append(y[None])
    return tuple(outs)
```

```python
import functools

import jax
import jax.numpy as jnp
from jax import lax
from jax.experimental import pallas as pl
from jax.experimental.pallas import tpu as pltpu

D_MODEL = 2048
DEPTH = 1
GRID_W = 64
HEAD_DIM = 128
NA_HEADS = 8
NA_WIDTH = NA_HEADS * HEAD_DIM
NA_KH = 8
NA_KW = 16
MLA_HEADS = 8
Q_LORA_RANK = 512
KV_LORA_RANK = 512
QK_NOPE_DIM = 128
QK_ROPE_DIM = 64
V_HEAD_DIM = 128
MLA_QK_DIM = QK_NOPE_DIM + QK_ROPE_DIM
MLA_WIDTH = MLA_HEADS * V_HEAD_DIM
D_FF = 5632
ROPE_THETA = 10000.0
RMS_EPS = 1e-6

PROJ_WIDTH = 3 * NA_WIDTH + Q_LORA_RANK + KV_LORA_RANK + 2 * QK_ROPE_DIM

V7X_VMEM_BYTES = 64 * 1024 * 1024
V7X_LANES = 128
NEG = -1e30
LOG2_E = 1.4426950408889634

BF16 = jnp.bfloat16
F32 = jnp.float32
_NT = (((1,), (1,)), ((), ()))


def _vmem_limit(*buffer_bytes):
    need = int(sum(buffer_bytes) * 1.5) + (4 << 20)
    return min(max(need, 32 << 20), V7X_VMEM_BYTES - (8 << 20))


def _params(semantics, *buffer_bytes):
    return pltpu.CompilerParams(dimension_semantics=semantics,
                                vmem_limit_bytes=_vmem_limit(*buffer_bytes))


def _rms(x, g):
    return x * lax.rsqrt(jnp.mean(x * x, axis=-1, keepdims=True) + RMS_EPS) * g


ADA_TN = 1024


def _adaln_kernel(c_ref, w_ref, b_ref, o_ref):
    c = c_ref[...]
    cs = c * jax.nn.sigmoid(c)
    w = w_ref[...]
    r0 = jnp.sum(w * cs[:, 0:1], axis=0, keepdims=True)
    r1 = jnp.sum(w * cs[:, 1:2], axis=0, keepdims=True)
    o_ref[...] = jnp.concatenate([r0, r1], axis=0) + b_ref[...]


def _adaln(c_cols, w, b):
    d, n = w.shape
    return pl.pallas_call(
        _adaln_kernel,
        grid=(n // ADA_TN,),
        in_specs=[pl.BlockSpec((d, 2), lambda j: (0, 0)),
                  pl.BlockSpec((d, ADA_TN), lambda j: (0, j)),
                  pl.BlockSpec((1, ADA_TN), lambda j: (0, j))],
        out_specs=pl.BlockSpec((2, ADA_TN), lambda j: (0, j)),
        out_shape=jax.ShapeDtypeStruct((2, n), F32),
        compiler_params=_params(("arbitrary",), 2 * d * ADA_TN * 4, 2 * d * V7X_LANES * 4),
        name="adaln",
    )(c_cols, w, b)


INPROJ_TM = 1024
INPROJ_TN = PROJ_WIDTH // 3


def _inproj_kernel(x_ref, g_ref, sc_ref, sh_ref, w_ref, o_ref, hn_ref):
    @pl.when(pl.program_id(1) == 0)
    def _():
        h = _rms(x_ref[...], g_ref[...]) * (1.0 + sc_ref[...]) + sh_ref[...]
        hn_ref[...] = h.astype(BF16)

    o_ref[...] = jnp.dot(hn_ref[...], w_ref[...], preferred_element_type=F32).astype(o_ref.dtype)


def _inproj(x, g, sc, sh, w):
    s, d = x.shape
    tm, tn = min(INPROJ_TM, s), INPROJ_TN
    row = lambda i, j: (0, 0)
    return pl.pallas_call(
        _inproj_kernel,
        grid=(s // tm, PROJ_WIDTH // tn),
        in_specs=[pl.BlockSpec((tm, d), lambda i, j: (i, 0)),
                  pl.BlockSpec((1, d), row), pl.BlockSpec((1, d), row), pl.BlockSpec((1, d), row),
                  pl.BlockSpec((d, tn), lambda i, j: (0, j))],
        out_specs=pl.BlockSpec((tm, tn), lambda i, j: (i, j)),
        out_shape=jax.ShapeDtypeStruct((s, PROJ_WIDTH), BF16),
        scratch_shapes=[pltpu.VMEM((tm, d), BF16)],
        compiler_params=_params(("parallel", "arbitrary"),
                                2 * tm * d * 4, tm * d * 2, 2 * d * tn * 2, 2 * tm * tn * 2),
        name="inproj",
    )(x, g, sc, sh, w)


NA_G = 8
NA_KR = NA_G + NA_KH
NA_KBLK = 4
NA_NKB = NA_KR // NA_KBLK
NA_HPS = 8
NA_PT_LEFT_MASKED = 14
NA_PT_RIGHT_MASKED = 15
NA_PT_ENTRIES = 16
RPB_ROWS = 2 * NA_KH - 1
RPB_COLS = 2 * NA_KW - 1


def _na_bias_kernel(rpb_ref, o_ref):
    h = pl.program_id(0)
    qc = lax.broadcasted_iota(jnp.int32, (GRID_W, 2 * GRID_W), 0)
    kk = lax.broadcasted_iota(jnp.int32, (GRID_W, 2 * GRID_W), 1)
    right = kk >= GRID_W
    kc = jnp.where(right, kk - GRID_W, kk)
    dc = kc - qc + (NA_KW - 1)
    c0 = jnp.clip(qc - NA_KW // 2, 0, GRID_W - NA_KW)
    col_ok = (kc >= c0) & (kc < c0 + NA_KW)

    def tile(dr_left, dr_right):
        acc = jnp.full((GRID_W, 2 * GRID_W), NEG, F32)
        for d in range(RPB_COLS):
            base = h * (RPB_ROWS * RPB_COLS) + d
            left = NEG if dr_left is None else rpb_ref[base + dr_left * RPB_COLS]
            rght = NEG if dr_right is None else rpb_ref[base + dr_right * RPB_COLS]
            acc = jnp.where(dc == d, jnp.where(right, rght, left), acc)
        return jnp.where(col_ok, acc, NEG)

    for e in range(NA_PT_LEFT_MASKED):
        o_ref[0, e] = tile(e, e + 1)
    o_ref[0, NA_PT_LEFT_MASKED] = tile(None, 3)
    o_ref[0, NA_PT_RIGHT_MASKED] = tile(10, None)


def _na_bias(rpb):
    return pl.pallas_call(
        _na_bias_kernel,
        grid=(NA_HEADS,),
        in_specs=[pl.BlockSpec(memory_space=pltpu.SMEM)],
        out_specs=pl.BlockSpec((1, NA_PT_ENTRIES, GRID_W, 2 * GRID_W), lambda h: (h, 0, 0, 0)),
        out_shape=jax.ShapeDtypeStruct((NA_HEADS, NA_PT_ENTRIES, GRID_W, 2 * GRID_W), F32),
        compiler_params=_params(("arbitrary",)),
        name="na_bias",
    )(rpb.reshape(-1))


def _na_row_plan(q_off):
    plan = []
    for i in range(NA_G):
        qr = q_off + i
        rs = min(max(qr - NA_KH // 2, 0), NA_KR - NA_KH)
        p_lo, p_hi = rs // 2, (rs + NA_KH - 1) // 2
        entries = []
        for p in range(p_lo, p_hi + 1):
            ok_l = rs <= 2 * p < rs + NA_KH
            ok_r = rs <= 2 * p + 1 < rs + NA_KH
            dr0 = 2 * p - qr + (NA_KH - 1)
            if ok_l and ok_r:
                assert 0 <= dr0 < NA_PT_LEFT_MASKED
                entries.append(dr0)
            elif ok_r:
                assert dr0 + 1 == 3
                entries.append(NA_PT_LEFT_MASKED)
            else:
                assert ok_l and dr0 == 10
                entries.append(NA_PT_RIGHT_MASKED)
        plan.append((p_lo, tuple(entries)))
    return plan


def _na_kernel(q_ref, k0, k1, k2, k3, v0, v1, v2, v3, pt_ref, o_ref, s_ref, p_ref, l_ref, *, n_groups):
    g, hp = pl.program_id(0), pl.program_id(1)
    last = n_groups - 1
    scale = HEAD_DIM ** -0.5

    def attend(q_off):
        plan = _na_row_plan(q_off)
        for hh in range(NA_HPS):
            lanes = slice(hh * HEAD_DIM, (hh + 1) * HEAD_DIM)
            q = (q_ref[:, lanes].astype(F32) * scale).astype(BF16)
            k = jnp.concatenate([r[:, lanes] for r in (k0, k1, k2, k3)], axis=0)
            s_ref[hh] = lax.dot_general(q, k, _NT, preferred_element_type=F32)
            p_ref[hh] = jnp.zeros(p_ref.shape[1:], p_ref.dtype)
        for hh in range(NA_HPS):
            h = hp * NA_HPS + hh
            for i, (p_lo, entries) in enumerate(plan):
                rows = slice(i * GRID_W, (i + 1) * GRID_W)
                cols = slice(p_lo * 2 * GRID_W, (p_lo + len(entries)) * 2 * GRID_W)
                bias = jnp.concatenate([pt_ref[h, e] for e in entries], axis=1)
                s = s_ref[hh, rows, cols] + bias
                p = jnp.exp(s - jnp.max(s, axis=-1, keepdims=True))
                l_ref[hh, rows, :] = jnp.sum(p, axis=-1, keepdims=True)
                p_ref[hh, rows, cols] = p.astype(BF16)
        for hh in range(NA_HPS):
            lanes = slice(hh * HEAD_DIM, (hh + 1) * HEAD_DIM)
            v = jnp.concatenate([r[:, lanes] for r in (v0, v1, v2, v3)], axis=0)
            o = jnp.dot(p_ref[hh], v, preferred_element_type=F32)
            o_ref[:, lanes] = o / l_ref[hh]

    pl.when(g == 0)(functools.partial(attend, 0))
    pl.when((g > 0) & (g < last))(functools.partial(attend, NA_KH // 2))
    pl.when(g == last)(functools.partial(attend, NA_KH))


def _na_attention(proj, pt):
    s = proj.shape[0]
    rows = s // GRID_W
    assert rows % NA_G == 0 and rows >= 2 * NA_G
    tq, tkb = NA_G * GRID_W, NA_KBLK * GRID_W
    n_kb = rows // NA_KBLK
    width = NA_HPS * HEAD_DIM
    k_col, v_col = NA_HEADS // NA_HPS, 2 * NA_HEADS // NA_HPS

    def kv_spec(j, col0):
        def index(g, hp):
            first = jnp.clip(g * (NA_G // NA_KBLK) - 1, 0, n_kb - NA_NKB)
            return (first + j, col0 + hp)
        return pl.BlockSpec((tkb, width), index)

    return pl.pallas_call(
        functools.partial(_na_kernel, n_groups=rows // NA_G),
        grid=(rows // NA_G, NA_HEADS // NA_HPS),
        in_specs=([pl.BlockSpec((tq, width), lambda g, hp: (g, hp))]
                  + [kv_spec(j, k_col) for j in range(NA_NKB)]
                  + [kv_spec(j, v_col) for j in range(NA_NKB)]
                  + [pl.BlockSpec(pt.shape, lambda g, hp: (0, 0, 0, 0))]),
        out_specs=pl.BlockSpec((tq, width), lambda g, hp: (g, hp)),
        out_shape=jax.ShapeDtypeStruct((s, NA_WIDTH), F32),
        scratch_shapes=[pltpu.VMEM((NA_HPS, tq, NA_KR * GRID_W), F32),
                        pltpu.VMEM((NA_HPS, tq, NA_KR * GRID_W), BF16),
                        pltpu.VMEM((NA_HPS, tq, 1), F32)],
        compiler_params=_params(("parallel", "arbitrary"),
                                2 * pt.size * 4, NA_HPS * tq * NA_KR * GRID_W * 6,
                                NA_HPS * tq * V7X_LANES * 4, 2 * 3 * tq * width * 4),
        name="na_attention",
    )(proj, *([proj] * (2 * NA_NKB)), pt)


MLAPREP_TM = 512
V_ONES_ROWS = 16


def _mla_prep_kernel(qc_ref, kvc_ref, kpe_ref, gq_ref, gkv_ref, wqn_ref, wqp_ref, wqr_ref,
                     wkn_ref, wv_ref, cs_ref, q_ref, k_ref, v_ref):
    scale = MLA_QK_DIM ** -0.5 * LOG2_E
    qn = _rms(qc_ref[...].astype(F32), gq_ref[...]).astype(BF16)
    kvn = _rms(kvc_ref[...].astype(F32), gkv_ref[...]).astype(BF16)

    cos_sin = cs_ref[...]
    prod = kpe_ref[...].astype(F32) * cos_sin
    k_pe = (prod + pltpu.roll(prod, QK_ROPE_DIM, 1))[:, :QK_ROPE_DIM].astype(BF16)

    sin_cos = pltpu.roll(cos_sin, QK_ROPE_DIM, 1)
    first_half = lax.broadcasted_iota(jnp.int32, cos_sin.shape, 1) < QK_ROPE_DIM
    cos = jnp.concatenate([jnp.where(first_half, cos_sin, sin_cos)] * (MLA_HEADS // 2), axis=1)
    sin = jnp.concatenate([jnp.where(first_half, sin_cos, cos_sin)] * (MLA_HEADS // 2), axis=1)
    q_pe = (jnp.dot(qn, wqp_ref[...], preferred_element_type=F32) * cos
            + jnp.dot(qn, wqr_ref[...], preferred_element_type=F32) * sin)
    q_pe_t = (q_pe * scale).T.astype(BF16)

    q_nope = jnp.dot(qn, wqn_ref[...], preferred_element_type=F32) * scale
    q_nope_t = q_nope.T.astype(BF16)
    k_nope = jnp.dot(kvn, wkn_ref[...], preferred_element_type=F32).astype(BF16)
    v_t = jnp.dot(kvn, wv_ref[...], preferred_element_type=F32).T.astype(BF16)
    for h in range(MLA_HEADS):
        nope = slice(h * QK_NOPE_DIM, (h + 1) * QK_NOPE_DIM)
        q_ref[h, :QK_NOPE_DIM, :] = q_nope_t[nope, :]
        q_ref[h, QK_NOPE_DIM:, :] = q_pe_t[h * QK_ROPE_DIM:(h + 1) * QK_ROPE_DIM, :]
        k_ref[h, :, :QK_NOPE_DIM] = k_nope[:, nope]
        k_ref[h, :, QK_NOPE_DIM:] = k_pe
        v_ref[h, 0, :V_HEAD_DIM, :] = v_t[h * V_HEAD_DIM:(h + 1) * V_HEAD_DIM, :]
        v_ref[h, 0, V_HEAD_DIM:, :] = jnp.ones((V_ONES_ROWS, v_t.shape[1]), BF16)


def _mla_prep(proj, g_q, g_kv, wqn, wqp, wqr, wkn, wv, cs):
    s = proj.shape[0]
    qk_width = MLA_QK_DIM
    tm = min(MLAPREP_TM, s)
    qc_blk = 3 * NA_WIDTH // Q_LORA_RANK
    kpe_blk = (3 * NA_WIDTH + Q_LORA_RANK + KV_LORA_RANK) // (2 * QK_ROPE_DIM)
    const = lambda i: (0, 0)
    return pl.pallas_call(
        _mla_prep_kernel,
        grid=(s // tm,),
        in_specs=[pl.BlockSpec((tm, Q_LORA_RANK), lambda i: (i, qc_blk)),
                  pl.BlockSpec((tm, KV_LORA_RANK), lambda i: (i, qc_blk + 1)),
                  pl.BlockSpec((tm, 2 * QK_ROPE_DIM), lambda i: (i, kpe_blk)),
                  pl.BlockSpec((1, Q_LORA_RANK), const), pl.BlockSpec((1, KV_LORA_RANK), const),
                  pl.BlockSpec(wqn.shape, const), pl.BlockSpec(wqp.shape, const),
                  pl.BlockSpec(wqr.shape, const), pl.BlockSpec(wkn.shape, const),
                  pl.BlockSpec(wv.shape, const),
                  pl.BlockSpec((tm, 2 * QK_ROPE_DIM), lambda i: (i, 0))],
        out_specs=[pl.BlockSpec((MLA_HEADS, qk_width, tm), lambda i: (0, 0, i)),
                   pl.BlockSpec((MLA_HEADS, tm, qk_width), lambda i: (0, i, 0)),
                   pl.BlockSpec((MLA_HEADS, 1, V_HEAD_DIM + V_ONES_ROWS, tm), lambda i: (0, i, 0, 0))],
        out_shape=[jax.ShapeDtypeStruct((MLA_HEADS, qk_width, s), BF16),
                   jax.ShapeDtypeStruct((MLA_HEADS, s, qk_width), BF16),
                   jax.ShapeDtypeStruct((MLA_HEADS, s // tm, V_HEAD_DIM + V_ONES_ROWS, tm), BF16)],
        compiler_params=_params(("parallel",), 2 * 2 * 512 * 4096, 2 * 3 * MLA_HEADS * tm * 256 * 2,
                                2 * tm * 1536 * 4, 8 * tm * 1024 * 4),
        name="mla_prep",
    )(proj, proj, proj, g_q, g_kv, wqn, wqp, wqr, wkn, wv, cs)


FLASH_TQ = 512
FLASH_SLOTS = 8


def _flash_kernel(qt_ref, k_ref, vt_ref, o_ref, acc_ref, s_ref, p_ref):
    tq = qt_ref.shape[2]
    n_slot, tk, _ = s_ref.shape
    tv = vt_ref.shape[3]
    n_sub = tk // tv
    n_k = vt_ref.shape[1] // n_sub
    assert n_k % n_slot == 0

    def scores(j):
        k = k_ref[0, pl.ds(pl.multiple_of(j * tk, tk), tk), :]
        return jnp.dot(k, qt_ref[0], preferred_element_type=F32)

    def accumulate(j, slot, alpha):
        pv = sum(jnp.dot(vt_ref[0, j * n_sub + u], p_ref[slot, u * tv:(u + 1) * tv, :],
                         preferred_element_type=F32) for u in range(n_sub))
        acc_ref[...] = alpha * acc_ref[...] + pv

    def step(j, slot, carry):
        m, alpha_prev, tile_max = carry
        s_next = scores(jnp.minimum(j + 1, n_k - 1))
        s_ref[(slot + 1) % n_slot] = s_next
        tile_max_next = jnp.max(s_next, axis=0, keepdims=True)
        accumulate(jnp.maximum(j - 1, 0), (slot - 1) % n_slot, alpha_prev)
        m_new = jnp.maximum(m, tile_max)
        alpha = jnp.exp2(m - m_new)
        p_ref[slot] = jnp.exp2((s_ref[slot] - m_new).astype(BF16))
        return m_new, alpha, tile_max_next

    def body(i, carry):
        for slot in range(n_slot):
            carry = step(n_slot * i + slot, slot, carry)
        return carry

    acc_ref[...] = jnp.zeros_like(acc_ref)
    p_ref[n_slot - 1] = jnp.zeros(p_ref.shape[1:], p_ref.dtype)
    s0 = scores(0)
    s_ref[0] = s0
    m0 = jnp.full((1, tq), -jnp.inf, F32)
    carry0 = (m0, jnp.ones((1, tq), F32), jnp.max(s0, axis=0, keepdims=True))
    _, alpha, _ = lax.fori_loop(0, n_k // n_slot, body, carry0)
    accumulate(n_k - 1, n_slot - 1, alpha)
    l = acc_ref[V_HEAD_DIM:V_HEAD_DIM + 1, :]
    o_ref[...] = (acc_ref[:V_HEAD_DIM, :] / l).T


def _flash(qt, k, vt):
    _, s, qk_width = k.shape
    tq, tk = min(FLASH_TQ, s), vt.shape[3]
    n_slot = min(FLASH_SLOTS, s // tk)
    return pl.pallas_call(
        _flash_kernel,
        grid=(MLA_HEADS, s // tq),
        in_specs=[pl.BlockSpec((1, qk_width, tq), lambda h, i: (h, 0, i)),
                  pl.BlockSpec((1, s, qk_width), lambda h, i: (h, 0, 0)),
                  pl.BlockSpec((1,) + vt.shape[1:], lambda h, i: (h, 0, 0, 0))],
        out_specs=pl.BlockSpec((tq, V_HEAD_DIM), lambda h, i: (i, h)),
        out_shape=jax.ShapeDtypeStruct((s, MLA_WIDTH), F32),
        scratch_shapes=[pltpu.VMEM((V_HEAD_DIM + V_ONES_ROWS, tq), F32),
                        pltpu.VMEM((n_slot, tk, tq), F32),
                        pltpu.VMEM((n_slot, tk, tq), BF16)],
        compiler_params=_params(("parallel", "arbitrary"),
                                2 * s * 256 * 2, 2 * s * V_HEAD_DIM * 2, n_slot * tk * tq * 6),
        name="mla_flash",
    )(qt, k, vt)


OUTPROJ_TM = 512


def _outproj_kernel(na_ref, mla_ref, x_ref, gna_ref, gmla_ref, gt_ref, g_ref, sc_ref, sh_ref, w_ref,
                    o_ref, hn_ref):
    na = _rms(na_ref[...], gna_ref[...]).astype(BF16)
    mla = _rms(mla_ref[...], gmla_ref[...]).astype(BF16)
    y = (jnp.dot(na, w_ref[:NA_WIDTH, :], preferred_element_type=F32)
         + jnp.dot(mla, w_ref[NA_WIDTH:, :], preferred_element_type=F32))
    x1 = x_ref[...] + gt_ref[...] * y
    o_ref[...] = x1
    hn_ref[...] = (_rms(x1, g_ref[...]) * (1.0 + sc_ref[...]) + sh_ref[...]).astype(BF16)


def _outproj(o_na, o_mla, x, g_na, g_mla, gt, g_ffn, sc, sh, w):
    s, d = x.shape
    tm = min(OUTPROJ_TM, s)
    const = lambda i: (0, 0)
    vec = pl.BlockSpec((1, d), const)
    return pl.pallas_call(
        _outproj_kernel,
        grid=(s // tm,),
        in_specs=[pl.BlockSpec((tm, NA_WIDTH), lambda i: (i, 0)),
                  pl.BlockSpec((tm, MLA_WIDTH), lambda i: (i, 0)),
                  pl.BlockSpec((tm, d), lambda i: (i, 0)),
                  pl.BlockSpec((1, NA_WIDTH), const), pl.BlockSpec((1, MLA_WIDTH), const),
                  vec, vec, vec, vec, pl.BlockSpec(w.shape, const)],
        out_specs=[pl.BlockSpec((tm, d), lambda i: (i, 0)), pl.BlockSpec((tm, d), lambda i: (i, 0))],
        out_shape=[jax.ShapeDtypeStruct((s, d), F32), jax.ShapeDtypeStruct((s, d), BF16)],
        compiler_params=_params(("parallel",), 2 * w.size * 2, 2 * 3 * tm * d * 4, 2 * tm * d * 2),
        name="outproj",
    )(o_na, o_mla, x, g_na, g_mla, gt, g_ffn, sc, sh, w)


FFN_TM = 512
FFN_TF = 512


def _ffn_kernel(x_ref, hn_ref, gt_ref, gfin_ref, wg_ref, wu_ref, wd_ref, o_ref, acc_ref, *, n_f):
    j = pl.program_id(1)

    @pl.when(j == 0)
    def _():
        acc_ref[...] = jnp.zeros_like(acc_ref)

    h = hn_ref[...]
    a = jnp.dot(h, wg_ref[...], preferred_element_type=F32)
    b = jnp.dot(h, wu_ref[...], preferred_element_type=F32)
    t = (a * jax.nn.sigmoid(a) * b).astype(BF16)
    acc_ref[...] += jnp.dot(t, wd_ref[...], preferred_element_type=F32)

    @pl.when(j == n_f - 1)
    def _():
        o_ref[...] = _rms(x_ref[...] + gt_ref[...] * acc_ref[...], gfin_ref[...])


def _ffn(x, hn, gt, g_final, wg, wu, wd):
    s, d = x.shape
    tm, tf = min(FFN_TM, s), FFN_TF
    n_f = D_FF // tf
    const = lambda i, j: (0, 0)
    vec = pl.BlockSpec((1, d), const)
    row = pl.BlockSpec((tm, d), lambda i, j: (i, 0))
    return pl.pallas_call(
        functools.partial(_ffn_kernel, n_f=n_f),
        grid=(s // tm, n_f),
        in_specs=[row, row, vec, vec,
                  pl.BlockSpec((d, tf), lambda i, j: (0, j)),
                  pl.BlockSpec((d, tf), lambda i, j: (0, j)),
                  pl.BlockSpec((tf, d), lambda i, j: (j, 0))],
        out_specs=row,
        out_shape=jax.ShapeDtypeStruct((s, d), F32),
        scratch_shapes=[pltpu.VMEM((tm, d), F32)],
        compiler_params=_params(("parallel", "arbitrary"),
                                4 * tm * d * 4, 2 * tm * d * 2, tm * d * 4, 2 * 3 * d * tf * 2,
                                3 * tm * tf * 4),
        name="ffn",
    )(x, hn, gt, g_final, wg, wu, wd)


def _rotate_half_cols(w):
    half = QK_ROPE_DIM // 2
    return jnp.concatenate([-w[..., half:], w[..., :half]], axis=-1)


def _rope_table(s):
    pos = jnp.arange(s, dtype=F32)
    inv_freq = ROPE_THETA ** (-jnp.arange(0, QK_ROPE_DIM, 2, dtype=F32) / QK_ROPE_DIM)
    ang = pos[:, None] * inv_freq[None, :]
    cos, sin = jnp.cos(ang), jnp.sin(ang)
    return jnp.concatenate([cos, cos, sin, sin], axis=-1)


def _layer_weights(l, w_in, w_uq, w_ukv, w_o, w_gate, w_up, w_down):
    k_pe0 = 3 * NA_WIDTH + Q_LORA_RANK + KV_LORA_RANK
    w_in_l = w_in[l]
    w_in_ext = jnp.concatenate([w_in_l, _rotate_half_cols(w_in_l[:, k_pe0:])], axis=1).astype(BF16)
    uq = w_uq[l].reshape(Q_LORA_RANK, MLA_HEADS, MLA_QK_DIM)
    wqn = uq[:, :, :QK_NOPE_DIM].reshape(Q_LORA_RANK, -1).astype(BF16)
    wqp = uq[:, :, QK_NOPE_DIM:]
    wqr = _rotate_half_cols(wqp).reshape(Q_LORA_RANK, -1).astype(BF16)
    wqp = wqp.reshape(Q_LORA_RANK, -1).astype(BF16)
    ukv = w_ukv[l].reshape(KV_LORA_RANK, MLA_HEADS, QK_NOPE_DIM + V_HEAD_DIM)
    wkn = ukv[:, :, :QK_NOPE_DIM].reshape(KV_LORA_RANK, -1).astype(BF16)
    wv = ukv[:, :, QK_NOPE_DIM:].reshape(KV_LORA_RANK, -1).astype(BF16)
    return dict(w_in=w_in_ext, wqn=wqn, wqp=wqp, wqr=wqr, wkn=wkn, wv=wv,
                w_o=w_o[l].astype(BF16), w_gate=w_gate[l].astype(BF16),
                w_up=w_up[l].astype(BF16), w_down=w_down[l].astype(BF16))


def _encoder_layer(x, mod, lw, pt, table, g_attn, g_q, g_kv, g_out_na, g_out_mla, g_ffn, g_final):
    d = D_MODEL
    sh1, sc1, gt1, sh2, sc2, gt2 = [mod[:, i * d:(i + 1) * d] for i in range(6)]
    proj = _inproj(x, g_attn, sc1, sh1, lw["w_in"])
    o_na = _na_attention(proj, pt)
    q, k, v = _mla_prep(proj, g_q, g_kv, lw["wqn"], lw["wqp"], lw["wqr"], lw["wkn"], lw["wv"], table)
    o_mla = _flash(q, k, v)
    x, hn = _outproj(o_na, o_mla, x, g_out_na, g_out_mla, gt1, g_ffn, sc2, sh2, lw["w_o"])
    return _ffn(x, hn, gt2, g_final, lw["w_gate"], lw["w_up"], lw["w_down"])


def kernel(x_prompt, x_sample, c_prompt, c_sample, w_ada, b_ada, g_attn, w_in, rpb, g_q, w_uq,
           g_kv, w_ukv, g_out_na, g_out_mla, w_o, g_ffn, w_gate, w_up, w_down, g_final):
    assert DEPTH == 1 and w_ada.shape[0] == DEPTH
    assert x_prompt.shape[0] == 1 and x_sample.shape[0] == 1
    l = 0
    c_cols = jnp.concatenate([c_prompt, c_sample], axis=0).T
    mod = _adaln(c_cols, w_ada[l], b_ada[l][None, :])
    lw = _layer_weights(l, w_in, w_uq, w_ukv, w_o, w_gate, w_up, w_down)
    pt = _na_bias(rpb[l])
    row = lambda a: a[None, :]
    table = _rope_table(max(x_prompt.shape[1], x_sample.shape[1]))
    outs = []
    for b, x in enumerate((x_prompt, x_sample)):
        y = _encoder_layer(x[0], mod[b:b + 1], lw, pt, table,
                           row(g_attn[l]), row(g_q[l]), row(g_kv[l]), row(g_out_na[l]),
                           row(g_out_mla[l]), row(g_ffn[l]), row(g_final))
        outs.append(y[None])
    return tuple(outs)
```

```python
import functools

import jax
import jax.numpy as jnp
from jax import lax
from jax.experimental import pallas as pl
from jax.experimental.pallas import tpu as pltpu

D_MODEL = 2048
DEPTH = 1
GRID_W = 64
HEAD_DIM = 128
NA_HEADS = 8
NA_WIDTH = NA_HEADS * HEAD_DIM
NA_KH = 8
NA_KW = 16
MLA_HEADS = 8
Q_LORA_RANK = 512
KV_LORA_RANK = 512
QK_NOPE_DIM = 128
QK_ROPE_DIM = 64
V_HEAD_DIM = 128
MLA_QK_DIM = QK_NOPE_DIM + QK_ROPE_DIM
MLA_WIDTH = MLA_HEADS * V_HEAD_DIM
D_FF = 5632
ROPE_THETA = 10000.0
RMS_EPS = 1e-6

PROJ_WIDTH = 3 * NA_WIDTH + Q_LORA_RANK + KV_LORA_RANK + 2 * QK_ROPE_DIM

V7X_VMEM_BYTES = 64 * 1024 * 1024
V7X_LANES = 128
NEG = -1e30
LOG2_E = 1.4426950408889634

BF16 = jnp.bfloat16
F32 = jnp.float32
_NT = (((1,), (1,)), ((), ()))


def _vmem_limit(*buffer_bytes):
    need = int(sum(buffer_bytes) * 1.5) + (4 << 20)
    return min(max(need, 32 << 20), V7X_VMEM_BYTES - (8 << 20))


def _params(semantics, *buffer_bytes):
    return pltpu.CompilerParams(dimension_semantics=semantics,
                                vmem_limit_bytes=_vmem_limit(*buffer_bytes))


def _rms(x, g):
    return x * lax.rsqrt(jnp.mean(x * x, axis=-1, keepdims=True) + RMS_EPS) * g


ADA_TN = 1024


def _adaln_kernel(c_ref, w_ref, b_ref, o_ref):
    c = c_ref[...]
    cs = c * jax.nn.sigmoid(c)
    w = w_ref[...]
    r0 = jnp.sum(w * cs[:, 0:1], axis=0, keepdims=True)
    r1 = jnp.sum(w * cs[:, 1:2], axis=0, keepdims=True)
    o_ref[...] = jnp.concatenate([r0, r1], axis=0) + b_ref[...]


def _adaln(c_cols, w, b):
    d, n = w.shape
    return pl.pallas_call(
        _adaln_kernel,
        grid=(n // ADA_TN,),
        in_specs=[pl.BlockSpec((d, 2), lambda j: (0, 0)),
                  pl.BlockSpec((d, ADA_TN), lambda j: (0, j)),
                  pl.BlockSpec((1, ADA_TN), lambda j: (0, j))],
        out_specs=pl.BlockSpec((2, ADA_TN), lambda j: (0, j)),
        out_shape=jax.ShapeDtypeStruct((2, n), F32),
        compiler_params=_params(("arbitrary",), 2 * d * ADA_TN * 4, 2 * d * V7X_LANES * 4),
        name="adaln",
    )(c_cols, w, b)


INPROJ_TM = 1024
INPROJ_TN = PROJ_WIDTH // 3


NORM_ROWS = 16


def _inproj_kernel(x_ref, g_ref, sc_ref, sh_ref, w_ref, o_ref, hn_ref):
    @pl.when(pl.program_id(1) == 0)
    def _():
        g, scale, shift = g_ref[...], 1.0 + sc_ref[...], sh_ref[...]
        for c in range(x_ref.shape[0] // NORM_ROWS):
            rows = slice(c * NORM_ROWS, (c + 1) * NORM_ROWS)
            hn_ref[rows, :] = (_rms(x_ref[rows, :], g) * scale + shift).astype(BF16)

    o_ref[...] = jnp.dot(hn_ref[...], w_ref[...], preferred_element_type=F32).astype(o_ref.dtype)


def _inproj(x, g, sc, sh, w):
    s, d = x.shape
    tm, tn = min(INPROJ_TM, s), INPROJ_TN
    row = lambda i, j: (0, 0)
    return pl.pallas_call(
        _inproj_kernel,
        grid=(s // tm, PROJ_WIDTH // tn),
        in_specs=[pl.BlockSpec((tm, d), lambda i, j: (i, 0)),
                  pl.BlockSpec((1, d), row), pl.BlockSpec((1, d), row), pl.BlockSpec((1, d), row),
                  pl.BlockSpec((d, tn), lambda i, j: (0, j))],
        out_specs=pl.BlockSpec((tm, tn), lambda i, j: (i, j)),
        out_shape=jax.ShapeDtypeStruct((s, PROJ_WIDTH), BF16),
        scratch_shapes=[pltpu.VMEM((tm, d), BF16)],
        compiler_params=_params(("parallel", "arbitrary"),
                                2 * tm * d * 4, tm * d * 2, 2 * d * tn * 2, 2 * tm * tn * 2),
        name="inproj",
    )(x, g, sc, sh, w)


NA_G = 8
NA_KR = NA_G + NA_KH
NA_KBLK = 4
NA_NKB = NA_KR // NA_KBLK
NA_HPS = 8
NA_PT_LEFT_MASKED = 14
NA_PT_RIGHT_MASKED = 15
NA_PT_ENTRIES = 16
RPB_ROWS = 2 * NA_KH - 1
RPB_COLS = 2 * NA_KW - 1


def _na_bias_kernel(rpb_ref, o_ref):
    h = pl.program_id(0)
    qc = lax.broadcasted_iota(jnp.int32, (GRID_W, 2 * GRID_W), 0)
    kk = lax.broadcasted_iota(jnp.int32, (GRID_W, 2 * GRID_W), 1)
    right = kk >= GRID_W
    kc = jnp.where(right, kk - GRID_W, kk)
    dc = kc - qc + (NA_KW - 1)
    c0 = jnp.clip(qc - NA_KW // 2, 0, GRID_W - NA_KW)
    col_ok = (kc >= c0) & (kc < c0 + NA_KW)

    def tile(dr_left, dr_right):
        acc = jnp.full((GRID_W, 2 * GRID_W), NEG, F32)
        for d in range(RPB_COLS):
            base = h * (RPB_ROWS * RPB_COLS) + d
            left = NEG if dr_left is None else rpb_ref[base + dr_left * RPB_COLS]
            rght = NEG if dr_right is None else rpb_ref[base + dr_right * RPB_COLS]
            acc = jnp.where(dc == d, jnp.where(right, rght, left), acc)
        return jnp.where(col_ok, acc, NEG)

    for e in range(NA_PT_LEFT_MASKED):
        o_ref[0, e] = tile(e, e + 1)
    o_ref[0, NA_PT_LEFT_MASKED] = tile(None, 3)
    o_ref[0, NA_PT_RIGHT_MASKED] = tile(10, None)


def _na_bias(rpb):
    return pl.pallas_call(
        _na_bias_kernel,
        grid=(NA_HEADS,),
        in_specs=[pl.BlockSpec(memory_space=pltpu.SMEM)],
        out_specs=pl.BlockSpec((1, NA_PT_ENTRIES, GRID_W, 2 * GRID_W), lambda h: (h, 0, 0, 0)),
        out_shape=jax.ShapeDtypeStruct((NA_HEADS, NA_PT_ENTRIES, GRID_W, 2 * GRID_W), F32),
        compiler_params=_params(("arbitrary",)),
        name="na_bias",
    )(rpb.reshape(-1))


def _na_row_plan(q_off):
    plan = []
    for i in range(NA_G):
        qr = q_off + i
        rs = min(max(qr - NA_KH // 2, 0), NA_KR - NA_KH)
        p_lo, p_hi = rs // 2, (rs + NA_KH - 1) // 2
        entries = []
        for p in range(p_lo, p_hi + 1):
            ok_l = rs <= 2 * p < rs + NA_KH
            ok_r = rs <= 2 * p + 1 < rs + NA_KH
            dr0 = 2 * p - qr + (NA_KH - 1)
            if ok_l and ok_r:
                assert 0 <= dr0 < NA_PT_LEFT_MASKED
                entries.append(dr0)
            elif ok_r:
                assert dr0 + 1 == 3
                entries.append(NA_PT_LEFT_MASKED)
            else:
                assert ok_l and dr0 == 10
                entries.append(NA_PT_RIGHT_MASKED)
        plan.append((p_lo, tuple(entries)))
    return plan


def _na_kernel(q_ref, k0, k1, k2, k3, v0, v1, v2, v3, pt_ref, o_ref, s_ref, p_ref, l_ref, *, n_groups):
    g, hp = pl.program_id(0), pl.program_id(1)
    last = n_groups - 1
    scale = HEAD_DIM ** -0.5

    def attend(q_off):
        plan = _na_row_plan(q_off)
        for hh in range(NA_HPS):
            lanes = slice(hh * HEAD_DIM, (hh + 1) * HEAD_DIM)
            q = (q_ref[:, lanes].astype(F32) * scale).astype(BF16)
            k = jnp.concatenate([r[:, lanes] for r in (k0, k1, k2, k3)], axis=0)
            s_ref[hh] = lax.dot_general(q, k, _NT, preferred_element_type=F32)
            p_ref[hh] = jnp.zeros(p_ref.shape[1:], p_ref.dtype)
        for hh in range(NA_HPS):
            h = hp * NA_HPS + hh
            for i, (p_lo, entries) in enumerate(plan):
                rows = slice(i * GRID_W, (i + 1) * GRID_W)
                cols = slice(p_lo * 2 * GRID_W, (p_lo + len(entries)) * 2 * GRID_W)
                bias = jnp.concatenate([pt_ref[h, e] for e in entries], axis=1)
                s = s_ref[hh, rows, cols] + bias
                p = jnp.exp(s - jnp.max(s, axis=-1, keepdims=True))
                l_ref[hh, rows, :] = jnp.sum(p, axis=-1, keepdims=True)
                p_ref[hh, rows, cols] = p.astype(BF16)
        for hh in range(NA_HPS):
            lanes = slice(hh * HEAD_DIM, (hh + 1) * HEAD_DIM)
            v = jnp.concatenate([r[:, lanes] for r in (v0, v1, v2, v3)], axis=0)
            o = jnp.dot(p_ref[hh], v, preferred_element_type=F32)
            o_ref[:, lanes] = o / l_ref[hh]

    pl.when(g == 0)(functools.partial(attend, 0))
    pl.when((g > 0) & (g < last))(functools.partial(attend, NA_KH // 2))
    pl.when(g == last)(functools.partial(attend, NA_KH))


def _na_attention(proj, pt):
    s = proj.shape[0]
    rows = s // GRID_W
    assert rows % NA_G == 0 and rows >= 2 * NA_G
    tq, tkb = NA_G * GRID_W, NA_KBLK * GRID_W
    n_kb = rows // NA_KBLK
    width = NA_HPS * HEAD_DIM
    k_col, v_col = NA_HEADS // NA_HPS, 2 * NA_HEADS // NA_HPS

    def kv_spec(j, col0):
        def index(g, hp):
            first = jnp.clip(g * (NA_G // NA_KBLK) - 1, 0, n_kb - NA_NKB)
            return (first + j, col0 + hp)
        return pl.BlockSpec((tkb, width), index)

    return pl.pallas_call(
        functools.partial(_na_kernel, n_groups=rows // NA_G),
        grid=(rows // NA_G, NA_HEADS // NA_HPS),
        in_specs=([pl.BlockSpec((tq, width), lambda g, hp: (g, hp))]
                  + [kv_spec(j, k_col) for j in range(NA_NKB)]
                  + [kv_spec(j, v_col) for j in range(NA_NKB)]
                  + [pl.BlockSpec(pt.shape, lambda g, hp: (0, 0, 0, 0))]),
        out_specs=pl.BlockSpec((tq, width), lambda g, hp: (g, hp)),
        out_shape=jax.ShapeDtypeStruct((s, NA_WIDTH), F32),
        scratch_shapes=[pltpu.VMEM((NA_HPS, tq, NA_KR * GRID_W), F32),
                        pltpu.VMEM((NA_HPS, tq, NA_KR * GRID_W), BF16),
                        pltpu.VMEM((NA_HPS, tq, 1), F32)],
        compiler_params=_params(("parallel", "arbitrary"),
                                2 * pt.size * 4, NA_HPS * tq * NA_KR * GRID_W * 6,
                                NA_HPS * tq * V7X_LANES * 4, 2 * 3 * tq * width * 4),
        name="na_attention",
    )(proj, *([proj] * (2 * NA_NKB)), pt)


MLAPREP_TM = 512
V_ONES_ROWS = 16


def _mla_prep_kernel(qc_ref, kvc_ref, kpe_ref, gq_ref, gkv_ref, wqn_ref, wqp_ref, wqr_ref,
                     wkn_ref, wv_ref, cs_ref, q_ref, k_ref, v_ref):
    scale = MLA_QK_DIM ** -0.5 * LOG2_E
    qn = _rms(qc_ref[...].astype(F32), gq_ref[...]).astype(BF16)
    kvn = _rms(kvc_ref[...].astype(F32), gkv_ref[...]).astype(BF16)

    cos_sin = cs_ref[...]
    prod = kpe_ref[...].astype(F32) * cos_sin
    k_pe = (prod + pltpu.roll(prod, QK_ROPE_DIM, 1))[:, :QK_ROPE_DIM].astype(BF16)

    sin_cos = pltpu.roll(cos_sin, QK_ROPE_DIM, 1)
    first_half = lax.broadcasted_iota(jnp.int32, cos_sin.shape, 1) < QK_ROPE_DIM
    cos = jnp.concatenate([jnp.where(first_half, cos_sin, sin_cos)] * (MLA_HEADS // 2), axis=1)
    sin = jnp.concatenate([jnp.where(first_half, sin_cos, cos_sin)] * (MLA_HEADS // 2), axis=1)
    q_pe = (jnp.dot(qn, wqp_ref[...], preferred_element_type=F32) * cos
            + jnp.dot(qn, wqr_ref[...], preferred_element_type=F32) * sin)
    q_pe_t = (q_pe * scale).T.astype(BF16)

    q_nope = jnp.dot(qn, wqn_ref[...], preferred_element_type=F32) * scale
    q_nope_t = q_nope.T.astype(BF16)
    k_nope = jnp.dot(kvn, wkn_ref[...], preferred_element_type=F32).astype(BF16)
    v_t = jnp.dot(kvn, wv_ref[...], preferred_element_type=F32).T.astype(BF16)
    for h in range(MLA_HEADS):
        nope = slice(h * QK_NOPE_DIM, (h + 1) * QK_NOPE_DIM)
        q_ref[h, :QK_NOPE_DIM, :] = q_nope_t[nope, :]
        q_ref[h, QK_NOPE_DIM:, :] = q_pe_t[h * QK_ROPE_DIM:(h + 1) * QK_ROPE_DIM, :]
        k_ref[h, :, :QK_NOPE_DIM] = k_nope[:, nope]
        k_ref[h, :, QK_NOPE_DIM:] = k_pe
        v_ref[h, 0, :V_HEAD_DIM, :] = v_t[h * V_HEAD_DIM:(h + 1) * V_HEAD_DIM, :]
        v_ref[h, 0, V_HEAD_DIM:, :] = jnp.ones((V_ONES_ROWS, v_t.shape[1]), BF16)


def _mla_prep(proj, g_q, g_kv, wqn, wqp, wqr, wkn, wv, cs):
    s = proj.shape[0]
    qk_width = MLA_QK_DIM
    tm = min(MLAPREP_TM, s)
    qc_blk = 3 * NA_WIDTH // Q_LORA_RANK
    kpe_blk = (3 * NA_WIDTH + Q_LORA_RANK + KV_LORA_RANK) // (2 * QK_ROPE_DIM)
    const = lambda i: (0, 0)
    return pl.pallas_call(
        _mla_prep_kernel,
        grid=(s // tm,),
        in_specs=[pl.BlockSpec((tm, Q_LORA_RANK), lambda i: (i, qc_blk)),
                  pl.BlockSpec((tm, KV_LORA_RANK), lambda i: (i, qc_blk + 1)),
                  pl.BlockSpec((tm, 2 * QK_ROPE_DIM), lambda i: (i, kpe_blk)),
                  pl.BlockSpec((1, Q_LORA_RANK), const), pl.BlockSpec((1, KV_LORA_RANK), const),
                  pl.BlockSpec(wqn.shape, const), pl.BlockSpec(wqp.shape, const),
                  pl.BlockSpec(wqr.shape, const), pl.BlockSpec(wkn.shape, const),
                  pl.BlockSpec(wv.shape, const),
                  pl.BlockSpec((tm, 2 * QK_ROPE_DIM), lambda i: (i, 0))],
        out_specs=[pl.BlockSpec((MLA_HEADS, qk_width, tm), lambda i: (0, 0, i)),
                   pl.BlockSpec((MLA_HEADS, tm, qk_width), lambda i: (0, i, 0)),
                   pl.BlockSpec((MLA_HEADS, 1, V_HEAD_DIM + V_ONES_ROWS, tm), lambda i: (0, i, 0, 0))],
        out_shape=[jax.ShapeDtypeStruct((MLA_HEADS, qk_width, s), BF16),
                   jax.ShapeDtypeStruct((MLA_HEADS, s, qk_width), BF16),
                   jax.ShapeDtypeStruct((MLA_HEADS, s // tm, V_HEAD_DIM + V_ONES_ROWS, tm), BF16)],
        compiler_params=_params(("parallel",), 2 * 2 * 512 * 4096, 2 * 3 * MLA_HEADS * tm * 256 * 2,
                                2 * tm * 1536 * 4, 8 * tm * 1024 * 4),
        name="mla_prep",
    )(proj, proj, proj, g_q, g_kv, wqn, wqp, wqr, wkn, wv, cs)


FLASH_TQ = 512
FLASH_SLOTS = 8


def _flash_kernel(qt_ref, k_ref, vt_ref, o_ref, acc_ref, s_ref, p_ref):
    tq = qt_ref.shape[2]
    n_slot, tk, _ = s_ref.shape
    tv = vt_ref.shape[3]
    n_sub = tk // tv
    n_k = vt_ref.shape[1] // n_sub
    assert n_k % n_slot == 0

    def scores(j):
        k = k_ref[0, pl.ds(pl.multiple_of(j * tk, tk), tk), :]
        return jnp.dot(k, qt_ref[0], preferred_element_type=F32)

    def accumulate(j, slot, alpha):
        pv = sum(jnp.dot(vt_ref[0, j * n_sub + u], p_ref[slot, u * tv:(u + 1) * tv, :],
                         preferred_element_type=F32) for u in range(n_sub))
        acc_ref[...] = alpha * acc_ref[...] + pv

    def step(j, slot, carry):
        m, alpha_prev, tile_max = carry
        s_next = scores(jnp.minimum(j + 1, n_k - 1))
        s_ref[(slot + 1) % n_slot] = s_next
        tile_max_next = jnp.max(s_next, axis=0, keepdims=True)
        accumulate(jnp.maximum(j - 1, 0), (slot - 1) % n_slot, alpha_prev)
        m_new = jnp.maximum(m, tile_max)
        alpha = jnp.exp2(m - m_new)
        p_ref[slot] = jnp.exp2((s_ref[slot] - m_new).astype(BF16))
        return m_new, alpha, tile_max_next

    def body(i, carry):
        for slot in range(n_slot):
            carry = step(n_slot * i + slot, slot, carry)
        return carry

    acc_ref[...] = jnp.zeros_like(acc_ref)
    p_ref[n_slot - 1] = jnp.zeros(p_ref.shape[1:], p_ref.dtype)
    s0 = scores(0)
    s_ref[0] = s0
    m0 = jnp.full((1, tq), -jnp.inf, F32)
    carry0 = (m0, jnp.ones((1, tq), F32), jnp.max(s0, axis=0, keepdims=True))
    _, alpha, _ = lax.fori_loop(0, n_k // n_slot, body, carry0)
    accumulate(n_k - 1, n_slot - 1, alpha)
    l = acc_ref[V_HEAD_DIM:V_HEAD_DIM + 1, :]
    o_ref[...] = (acc_ref[:V_HEAD_DIM, :] / l).T


def _flash(qt, k, vt):
    _, s, qk_width = k.shape
    tq, tk = min(FLASH_TQ, s), vt.shape[3]
    n_slot = min(FLASH_SLOTS, s // tk)
    return pl.pallas_call(
        _flash_kernel,
        grid=(MLA_HEADS, s // tq),
        in_specs=[pl.BlockSpec((1, qk_width, tq), lambda h, i: (h, 0, i)),
                  pl.BlockSpec((1, s, qk_width), lambda h, i: (h, 0, 0)),
                  pl.BlockSpec((1,) + vt.shape[1:], lambda h, i: (h, 0, 0, 0))],
        out_specs=pl.BlockSpec((tq, V_HEAD_DIM), lambda h, i: (i, h)),
        out_shape=jax.ShapeDtypeStruct((s, MLA_WIDTH), F32),
        scratch_shapes=[pltpu.VMEM((V_HEAD_DIM + V_ONES_ROWS, tq), F32),
                        pltpu.VMEM((n_slot, tk, tq), F32),
                        pltpu.VMEM((n_slot, tk, tq), BF16)],
        compiler_params=_params(("parallel", "arbitrary"),
                                2 * s * 256 * 2, 2 * s * V_HEAD_DIM * 2, n_slot * tk * tq * 6),
        name="mla_flash",
    )(qt, k, vt)


OUTPROJ_TM = 512


def _outproj_kernel(na_ref, mla_ref, x_ref, gna_ref, gmla_ref, gt_ref, g_ref, sc_ref, sh_ref, w_ref,
                    o_ref, hn_ref):
    na = _rms(na_ref[...], gna_ref[...]).astype(BF16)
    mla = _rms(mla_ref[...], gmla_ref[...]).astype(BF16)
    y = (jnp.dot(na, w_ref[:NA_WIDTH, :], preferred_element_type=F32)
         + jnp.dot(mla, w_ref[NA_WIDTH:, :], preferred_element_type=F32))
    x1 = x_ref[...] + gt_ref[...] * y
    o_ref[...] = x1
    hn_ref[...] = (_rms(x1, g_ref[...]) * (1.0 + sc_ref[...]) + sh_ref[...]).astype(BF16)


def _outproj(o_na, o_mla, x, g_na, g_mla, gt, g_ffn, sc, sh, w):
    s, d = x.shape
    tm = min(OUTPROJ_TM, s)
    const = lambda i: (0, 0)
    vec = pl.BlockSpec((1, d), const)
    return pl.pallas_call(
        _outproj_kernel,
        grid=(s // tm,),
        in_specs=[pl.BlockSpec((tm, NA_WIDTH), lambda i: (i, 0)),
                  pl.BlockSpec((tm, MLA_WIDTH), lambda i: (i, 0)),
                  pl.BlockSpec((tm, d), lambda i: (i, 0)),
                  pl.BlockSpec((1, NA_WIDTH), const), pl.BlockSpec((1, MLA_WIDTH), const),
                  vec, vec, vec, vec, pl.BlockSpec(w.shape, const)],
        out_specs=[pl.BlockSpec((tm, d), lambda i: (i, 0)), pl.BlockSpec((tm, d), lambda i: (i, 0))],
        out_shape=[jax.ShapeDtypeStruct((s, d), F32), jax.ShapeDtypeStruct((s, d), BF16)],
        compiler_params=_params(("parallel",), 2 * w.size * 2, 2 * 3 * tm * d * 4, 2 * tm * d * 2),
        name="outproj",
    )(o_na, o_mla, x, g_na, g_mla, gt, g_ffn, sc, sh, w)


FFN_TM = 512
FFN_TF = 512


def _ffn_kernel(x_ref, hn_ref, gt_ref, gfin_ref, wg_ref, wu_ref, wd_ref, o_ref, acc_ref, *, n_f):
    j = pl.program_id(1)

    @pl.when(j == 0)
    def _():
        acc_ref[...] = jnp.zeros_like(acc_ref)

    h = hn_ref[...]
    a = jnp.dot(h, wg_ref[...], preferred_element_type=F32)
    b = jnp.dot(h, wu_ref[...], preferred_element_type=F32)
    t = (a * jax.nn.sigmoid(a) * b).astype(BF16)
    acc_ref[...] += jnp.dot(t, wd_ref[...], preferred_element_type=F32)

    @pl.when(j == n_f - 1)
    def _():
        gt, gfin = gt_ref[...], gfin_ref[...]
        for c in range(x_ref.shape[0] // NORM_ROWS):
            rows = slice(c * NORM_ROWS, (c + 1) * NORM_ROWS)
            o_ref[rows, :] = _rms(x_ref[rows, :] + gt * acc_ref[rows, :], gfin)


def _ffn(x, hn, gt, g_final, wg, wu, wd):
    s, d = x.shape
    tm, tf = min(FFN_TM, s), FFN_TF
    n_f = D_FF // tf
    const = lambda i, j: (0, 0)
    vec = pl.BlockSpec((1, d), const)
    row = pl.BlockSpec((tm, d), lambda i, j: (i, 0))
    return pl.pallas_call(
        functools.partial(_ffn_kernel, n_f=n_f),
        grid=(s // tm, n_f),
        in_specs=[row, row, vec, vec,
                  pl.BlockSpec((d, tf), lambda i, j: (0, j)),
                  pl.BlockSpec((d, tf), lambda i, j: (0, j)),
                  pl.BlockSpec((tf, d), lambda i, j: (j, 0))],
        out_specs=row,
        out_shape=jax.ShapeDtypeStruct((s, d), F32),
        scratch_shapes=[pltpu.VMEM((tm, d), F32)],
        compiler_params=_params(("parallel", "arbitrary"),
                                4 * tm * d * 4, 2 * tm * d * 2, tm * d * 4, 2 * 3 * d * tf * 2,
                                3 * tm * tf * 4),
        name="ffn",
    )(x, hn, gt, g_final, wg, wu, wd)


def _rotate_half_cols(w):
    half = QK_ROPE_DIM // 2
    return jnp.concatenate([-w[..., half:], w[..., :half]], axis=-1)


def _rope_table(s):
    pos = jnp.arange(s, dtype=F32)
    inv_freq = ROPE_THETA ** (-jnp.arange(0, QK_ROPE_DIM, 2, dtype=F32) / QK_ROPE_DIM)
    ang = pos[:, None] * inv_freq[None, :]
    cos, sin = jnp.cos(ang), jnp.sin(ang)
    return jnp.concatenate([cos, cos, sin, sin], axis=-1)


def _layer_weights(l, w_in, w_uq, w_ukv, w_o, w_gate, w_up, w_down):
    k_pe0 = 3 * NA_WIDTH + Q_LORA_RANK + KV_LORA_RANK
    w_in_l = w_in[l]
    w_in_ext = jnp.concatenate([w_in_l, _rotate_half_cols(w_in_l[:, k_pe0:])], axis=1).astype(BF16)
    uq = w_uq[l].reshape(Q_LORA_RANK, MLA_HEADS, MLA_QK_DIM)
    wqn = uq[:, :, :QK_NOPE_DIM].reshape(Q_LORA_RANK, -1).astype(BF16)
    wqp = uq[:, :, QK_NOPE_DIM:]
    wqr = _rotate_half_cols(wqp).reshape(Q_LORA_RANK, -1).astype(BF16)
    wqp = wqp.reshape(Q_LORA_RANK, -1).astype(BF16)
    ukv = w_ukv[l].reshape(KV_LORA_RANK, MLA_HEADS, QK_NOPE_DIM + V_HEAD_DIM)
    wkn = ukv[:, :, :QK_NOPE_DIM].reshape(KV_LORA_RANK, -1).astype(BF16)
    wv = ukv[:, :, QK_NOPE_DIM:].reshape(KV_LORA_RANK, -1).astype(BF16)
    return dict(w_in=w_in_ext, wqn=wqn, wqp=wqp, wqr=wqr, wkn=wkn, wv=wv,
                w_o=w_o[l].astype(BF16), w_gate=w_gate[l].astype(BF16),
                w_up=w_up[l].astype(BF16), w_down=w_down[l].astype(BF16))


def _encoder_layer(x, mod, lw, pt, table, g_attn, g_q, g_kv, g_out_na, g_out_mla, g_ffn, g_final):
    d = D_MODEL
    sh1, sc1, gt1, sh2, sc2, gt2 = [mod[:, i * d:(i + 1) * d] for i in range(6)]
    proj = _inproj(x, g_attn, sc1, sh1, lw["w_in"])
    o_na = _na_attention(proj, pt)
    q, k, v = _mla_prep(proj, g_q, g_kv, lw["wqn"], lw["wqp"], lw["wqr"], lw["wkn"], lw["wv"], table)
    o_mla = _flash(q, k, v)
    x, hn = _outproj(o_na, o_mla, x, g_out_na, g_out_mla, gt1, g_ffn, sc2, sh2, lw["w_o"])
    return _ffn(x, hn, gt2, g_final, lw["w_gate"], lw["w_up"], lw["w_down"])


def kernel(x_prompt, x_sample, c_prompt, c_sample, w_ada, b_ada, g_attn, w_in, rpb, g_q, w_uq,
           g_kv, w_ukv, g_out_na, g_out_mla, w_o, g_ffn, w_gate, w_up, w_down, g_final):
    assert DEPTH == 1 and w_ada.shape[0] == DEPTH
    assert x_prompt.shape[0] == 1 and x_sample.shape[0] == 1
    l = 0
    c_cols = jnp.concatenate([c_prompt, c_sample], axis=0).T
    mod = _adaln(c_cols, w_ada[l], b_ada[l][None, :])
    lw = _layer_weights(l, w_in, w_uq, w_ukv, w_o, w_gate, w_up, w_down)
    pt = _na_bias(rpb[l])
    row = lambda a: a[None, :]
    table = _rope_table(max(x_prompt.shape[1], x_sample.shape[1]))
    outs = []
    for b, x in enumerate((x_prompt, x_sample)):
        y = _encoder_layer(x[0], mod[b:b + 1], lw, pt, table,
                           row(g_attn[l]), row(g_q[l]), row(g_kv[l]), row(g_out_na[l]),
                           row(g_out_mla[l]), row(g_ffn[l]), row(g_final))
        outs.append(y[None])
    return tuple(outs)
```

```python
import functools

import jax
import jax.numpy as jnp
from jax import lax
from jax.experimental import pallas as pl
from jax.experimental.pallas import tpu as pltpu

D_MODEL = 2048
DEPTH = 1
GRID_W = 64
HEAD_DIM = 128
NA_HEADS = 8
NA_WIDTH = NA_HEADS * HEAD_DIM
NA_KH = 8
NA_KW = 16
MLA_HEADS = 8
Q_LORA_RANK = 512
KV_LORA_RANK = 512
QK_NOPE_DIM = 128
QK_ROPE_DIM = 64
V_HEAD_DIM = 128
MLA_QK_DIM = QK_NOPE_DIM + QK_ROPE_DIM
MLA_WIDTH = MLA_HEADS * V_HEAD_DIM
D_FF = 5632
ROPE_THETA = 10000.0
RMS_EPS = 1e-6

PROJ_WIDTH = 3 * NA_WIDTH + Q_LORA_RANK + KV_LORA_RANK + 2 * QK_ROPE_DIM

V7X_VMEM_BYTES = 64 * 1024 * 1024
V7X_LANES = 128
NEG = -1e30
LOG2_E = 1.4426950408889634

BF16 = jnp.bfloat16
F32 = jnp.float32
_NT = (((1,), (1,)), ((), ()))


def _vmem_limit(*buffer_bytes):
    need = int(sum(buffer_bytes) * 1.5) + (4 << 20)
    return min(max(need, 32 << 20), V7X_VMEM_BYTES - (8 << 20))


def _params(semantics, *buffer_bytes):
    return pltpu.CompilerParams(dimension_semantics=semantics,
                                vmem_limit_bytes=_vmem_limit(*buffer_bytes))


def _rms(x, g):
    return x * lax.rsqrt(jnp.mean(x * x, axis=-1, keepdims=True) + RMS_EPS) * g


ADA_TN = 1024


def _adaln_kernel(c_ref, w_ref, b_ref, o_ref):
    c = c_ref[...]
    cs = c * jax.nn.sigmoid(c)
    w = w_ref[...]
    r0 = jnp.sum(w * cs[:, 0:1], axis=0, keepdims=True)
    r1 = jnp.sum(w * cs[:, 1:2], axis=0, keepdims=True)
    o_ref[...] = jnp.concatenate([r0, r1], axis=0) + b_ref[...]


def _adaln(c_cols, w, b):
    d, n = w.shape
    return pl.pallas_call(
        _adaln_kernel,
        grid=(n // ADA_TN,),
        in_specs=[pl.BlockSpec((d, 2), lambda j: (0, 0)),
                  pl.BlockSpec((d, ADA_TN), lambda j: (0, j)),
                  pl.BlockSpec((1, ADA_TN), lambda j: (0, j))],
        out_specs=pl.BlockSpec((2, ADA_TN), lambda j: (0, j)),
        out_shape=jax.ShapeDtypeStruct((2, n), F32),
        compiler_params=_params(("arbitrary",), 2 * d * ADA_TN * 4, 2 * d * V7X_LANES * 4),
        name="adaln",
    )(c_cols, w, b)


INPROJ_TM = 1024
INPROJ_TN = PROJ_WIDTH // 3


NORM_ROWS = 16


def _inproj_kernel(x_ref, g_ref, sc_ref, sh_ref, w_ref, o_ref, hn_ref):
    @pl.when(pl.program_id(1) == 0)
    def _():
        g, scale, shift = g_ref[...], 1.0 + sc_ref[...], sh_ref[...]
        for c in range(x_ref.shape[0] // NORM_ROWS):
            rows = slice(c * NORM_ROWS, (c + 1) * NORM_ROWS)
            hn_ref[rows, :] = (_rms(x_ref[rows, :], g) * scale + shift).astype(BF16)

    o_ref[...] = jnp.dot(hn_ref[...], w_ref[...], preferred_element_type=F32).astype(o_ref.dtype)


def _inproj(x, g, sc, sh, w):
    s, d = x.shape
    tm, tn = min(INPROJ_TM, s), INPROJ_TN
    row = lambda i, j: (0, 0)
    return pl.pallas_call(
        _inproj_kernel,
        grid=(s // tm, PROJ_WIDTH // tn),
        in_specs=[pl.BlockSpec((tm, d), lambda i, j: (i, 0)),
                  pl.BlockSpec((1, d), row), pl.BlockSpec((1, d), row), pl.BlockSpec((1, d), row),
                  pl.BlockSpec((d, tn), lambda i, j: (0, j))],
        out_specs=pl.BlockSpec((tm, tn), lambda i, j: (i, j)),
        out_shape=jax.ShapeDtypeStruct((s, PROJ_WIDTH), BF16),
        scratch_shapes=[pltpu.VMEM((tm, d), BF16)],
        compiler_params=_params(("parallel", "arbitrary"),
                                2 * tm * d * 4, tm * d * 2, 2 * d * tn * 2, 2 * tm * tn * 2),
        name="inproj",
    )(x, g, sc, sh, w)


NA_G = 8
NA_KR = NA_G + NA_KH
NA_KBLK = 4
NA_NKB = NA_KR // NA_KBLK
NA_HPS = 8
NA_PT_LEFT_MASKED = 14
NA_PT_RIGHT_MASKED = 15
NA_PT_ENTRIES = 16
RPB_ROWS = 2 * NA_KH - 1
RPB_COLS = 2 * NA_KW - 1


def _na_bias_kernel(rpb_ref, o_ref):
    h = pl.program_id(0)
    qc = lax.broadcasted_iota(jnp.int32, (GRID_W, 2 * GRID_W), 0)
    kk = lax.broadcasted_iota(jnp.int32, (GRID_W, 2 * GRID_W), 1)
    right = kk >= GRID_W
    kc = jnp.where(right, kk - GRID_W, kk)
    dc = kc - qc + (NA_KW - 1)
    c0 = jnp.clip(qc - NA_KW // 2, 0, GRID_W - NA_KW)
    col_ok = (kc >= c0) & (kc < c0 + NA_KW)

    def tile(dr_left, dr_right):
        acc = jnp.full((GRID_W, 2 * GRID_W), NEG, F32)
        for d in range(RPB_COLS):
            base = h * (RPB_ROWS * RPB_COLS) + d
            left = NEG if dr_left is None else rpb_ref[base + dr_left * RPB_COLS]
            rght = NEG if dr_right is None else rpb_ref[base + dr_right * RPB_COLS]
            acc = jnp.where(dc == d, jnp.where(right, rght, left), acc)
        return jnp.where(col_ok, acc, NEG)

    for e in range(NA_PT_LEFT_MASKED):
        o_ref[0, e] = tile(e, e + 1)
    o_ref[0, NA_PT_LEFT_MASKED] = tile(None, 3)
    o_ref[0, NA_PT_RIGHT_MASKED] = tile(10, None)


def _na_bias(rpb):
    return pl.pallas_call(
        _na_bias_kernel,
        grid=(NA_HEADS,),
        in_specs=[pl.BlockSpec(memory_space=pltpu.SMEM)],
        out_specs=pl.BlockSpec((1, NA_PT_ENTRIES, GRID_W, 2 * GRID_W), lambda h: (h, 0, 0, 0)),
        out_shape=jax.ShapeDtypeStruct((NA_HEADS, NA_PT_ENTRIES, GRID_W, 2 * GRID_W), F32),
        compiler_params=_params(("arbitrary",)),
        name="na_bias",
    )(rpb.reshape(-1))


def _na_row_plan(q_off):
    plan = []
    for i in range(NA_G):
        qr = q_off + i
        rs = min(max(qr - NA_KH // 2, 0), NA_KR - NA_KH)
        p_lo, p_hi = rs // 2, (rs + NA_KH - 1) // 2
        entries = []
        for p in range(p_lo, p_hi + 1):
            ok_l = rs <= 2 * p < rs + NA_KH
            ok_r = rs <= 2 * p + 1 < rs + NA_KH
            dr0 = 2 * p - qr + (NA_KH - 1)
            if ok_l and ok_r:
                assert 0 <= dr0 < NA_PT_LEFT_MASKED
                entries.append(dr0)
            elif ok_r:
                assert dr0 + 1 == 3
                entries.append(NA_PT_LEFT_MASKED)
            else:
                assert ok_l and dr0 == 10
                entries.append(NA_PT_RIGHT_MASKED)
        plan.append((p_lo, tuple(entries)))
    return plan


def _na_kernel(q_ref, k0, k1, k2, k3, v0, v1, v2, v3, pt_ref, o_ref, s_ref, p_ref, l_ref, *, n_groups):
    g, hp = pl.program_id(0), pl.program_id(1)
    last = n_groups - 1
    scale = HEAD_DIM ** -0.5

    def attend(q_off):
        plan = _na_row_plan(q_off)
        for hh in range(NA_HPS):
            lanes = slice(hh * HEAD_DIM, (hh + 1) * HEAD_DIM)
            q = (q_ref[:, lanes].astype(F32) * scale).astype(BF16)
            k = jnp.concatenate([r[:, lanes] for r in (k0, k1, k2, k3)], axis=0)
            s_ref[hh] = lax.dot_general(q, k, _NT, preferred_element_type=F32)
            p_ref[hh] = jnp.zeros(p_ref.shape[1:], p_ref.dtype)
        for hh in range(NA_HPS):
            h = hp * NA_HPS + hh
            for i, (p_lo, entries) in enumerate(plan):
                rows = slice(i * GRID_W, (i + 1) * GRID_W)
                cols = slice(p_lo * 2 * GRID_W, (p_lo + len(entries)) * 2 * GRID_W)
                bias = jnp.concatenate([pt_ref[h, e] for e in entries], axis=1)
                s = s_ref[hh, rows, cols] + bias
                p = jnp.exp(s - jnp.max(s, axis=-1, keepdims=True))
                l_ref[hh, rows, :] = jnp.sum(p, axis=-1, keepdims=True)
                p_ref[hh, rows, cols] = p.astype(BF16)
        for hh in range(NA_HPS):
            lanes = slice(hh * HEAD_DIM, (hh + 1) * HEAD_DIM)
            v = jnp.concatenate([r[:, lanes] for r in (v0, v1, v2, v3)], axis=0)
            o = jnp.dot(p_ref[hh], v, preferred_element_type=F32)
            o_ref[:, lanes] = o / l_ref[hh]

    pl.when(g == 0)(functools.partial(attend, 0))
    pl.when((g > 0) & (g < last))(functools.partial(attend, NA_KH // 2))
    pl.when(g == last)(functools.partial(attend, NA_KH))


def _na_attention(proj, pt):
    s = proj.shape[0]
    rows = s // GRID_W
    assert rows % NA_G == 0 and rows >= 2 * NA_G
    tq, tkb = NA_G * GRID_W, NA_KBLK * GRID_W
    n_kb = rows // NA_KBLK
    width = NA_HPS * HEAD_DIM
    k_col, v_col = NA_HEADS // NA_HPS, 2 * NA_HEADS // NA_HPS

    def kv_spec(j, col0):
        def index(g, hp):
            first = jnp.clip(g * (NA_G // NA_KBLK) - 1, 0, n_kb - NA_NKB)
            return (first + j, col0 + hp)
        return pl.BlockSpec((tkb, width), index)

    return pl.pallas_call(
        functools.partial(_na_kernel, n_groups=rows // NA_G),
        grid=(rows // NA_G, NA_HEADS // NA_HPS),
        in_specs=([pl.BlockSpec((tq, width), lambda g, hp: (g, hp))]
                  + [kv_spec(j, k_col) for j in range(NA_NKB)]
                  + [kv_spec(j, v_col) for j in range(NA_NKB)]
                  + [pl.BlockSpec(pt.shape, lambda g, hp: (0, 0, 0, 0))]),
        out_specs=pl.BlockSpec((tq, width), lambda g, hp: (g, hp)),
        out_shape=jax.ShapeDtypeStruct((s, NA_WIDTH), F32),
        scratch_shapes=[pltpu.VMEM((NA_HPS, tq, NA_KR * GRID_W), F32),
                        pltpu.VMEM((NA_HPS, tq, NA_KR * GRID_W), BF16),
                        pltpu.VMEM((NA_HPS, tq, 1), F32)],
        compiler_params=_params(("parallel", "arbitrary"),
                                2 * pt.size * 4, NA_HPS * tq * NA_KR * GRID_W * 6,
                                NA_HPS * tq * V7X_LANES * 4, 2 * 3 * tq * width * 4),
        name="na_attention",
    )(proj, *([proj] * (2 * NA_NKB)), pt)


MLAPREP_TM = 512
V_ONES_ROWS = 16


def _mla_prep_kernel(qc_ref, kvc_ref, kpe_ref, gq_ref, gkv_ref, wqn_ref, wqp_ref, wqr_ref,
                     wkn_ref, wv_ref, cs_ref, q_ref, k_ref, v_ref):
    scale = MLA_QK_DIM ** -0.5 * LOG2_E
    qn = _rms(qc_ref[...].astype(F32), gq_ref[...]).astype(BF16)
    kvn = _rms(kvc_ref[...].astype(F32), gkv_ref[...]).astype(BF16)

    cos_sin = cs_ref[...]
    prod = kpe_ref[...].astype(F32) * cos_sin
    k_pe = (prod + pltpu.roll(prod, QK_ROPE_DIM, 1))[:, :QK_ROPE_DIM].astype(BF16)

    sin_cos = pltpu.roll(cos_sin, QK_ROPE_DIM, 1)
    first_half = lax.broadcasted_iota(jnp.int32, cos_sin.shape, 1) < QK_ROPE_DIM
    cos = jnp.concatenate([jnp.where(first_half, cos_sin, sin_cos)] * (MLA_HEADS // 2), axis=1)
    sin = jnp.concatenate([jnp.where(first_half, sin_cos, cos_sin)] * (MLA_HEADS // 2), axis=1)
    q_pe = (jnp.dot(qn, wqp_ref[...], preferred_element_type=F32) * cos
            + jnp.dot(qn, wqr_ref[...], preferred_element_type=F32) * sin)
    q_pe_t = (q_pe * scale).T.astype(BF16)

    q_nope = jnp.dot(qn, wqn_ref[...], preferred_element_type=F32) * scale
    q_nope_t = q_nope.T.astype(BF16)
    k_nope = jnp.dot(kvn, wkn_ref[...], preferred_element_type=F32).astype(BF16)
    v_t = jnp.dot(kvn, wv_ref[...], preferred_element_type=F32).T.astype(BF16)
    for h in range(MLA_HEADS):
        nope = slice(h * QK_NOPE_DIM, (h + 1) * QK_NOPE_DIM)
        q_ref[h, :QK_NOPE_DIM, :] = q_nope_t[nope, :]
        q_ref[h, QK_NOPE_DIM:, :] = q_pe_t[h * QK_ROPE_DIM:(h + 1) * QK_ROPE_DIM, :]
        k_ref[h, :, :QK_NOPE_DIM] = k_nope[:, nope]
        k_ref[h, :, QK_NOPE_DIM:] = k_pe
        v_ref[h, 0, :V_HEAD_DIM, :] = v_t[h * V_HEAD_DIM:(h + 1) * V_HEAD_DIM, :]
        v_ref[h, 0, V_HEAD_DIM:, :] = jnp.ones((V_ONES_ROWS, v_t.shape[1]), BF16)


def _mla_prep(proj, g_q, g_kv, wqn, wqp, wqr, wkn, wv, cs):
    s = proj.shape[0]
    qk_width = MLA_QK_DIM
    tm = min(MLAPREP_TM, s)
    qc_blk = 3 * NA_WIDTH // Q_LORA_RANK
    kpe_blk = (3 * NA_WIDTH + Q_LORA_RANK + KV_LORA_RANK) // (2 * QK_ROPE_DIM)
    const = lambda i: (0, 0)
    return pl.pallas_call(
        _mla_prep_kernel,
        grid=(s // tm,),
        in_specs=[pl.BlockSpec((tm, Q_LORA_RANK), lambda i: (i, qc_blk)),
                  pl.BlockSpec((tm, KV_LORA_RANK), lambda i: (i, qc_blk + 1)),
                  pl.BlockSpec((tm, 2 * QK_ROPE_DIM), lambda i: (i, kpe_blk)),
                  pl.BlockSpec((1, Q_LORA_RANK), const), pl.BlockSpec((1, KV_LORA_RANK), const),
                  pl.BlockSpec(wqn.shape, const), pl.BlockSpec(wqp.shape, const),
                  pl.BlockSpec(wqr.shape, const), pl.BlockSpec(wkn.shape, const),
                  pl.BlockSpec(wv.shape, const),
                  pl.BlockSpec((tm, 2 * QK_ROPE_DIM), lambda i: (i, 0))],
        out_specs=[pl.BlockSpec((MLA_HEADS, qk_width, tm), lambda i: (0, 0, i)),
                   pl.BlockSpec((MLA_HEADS, tm, qk_width), lambda i: (0, i, 0)),
                   pl.BlockSpec((MLA_HEADS, 1, V_HEAD_DIM + V_ONES_ROWS, tm), lambda i: (0, i, 0, 0))],
        out_shape=[jax.ShapeDtypeStruct((MLA_HEADS, qk_width, s), BF16),
                   jax.ShapeDtypeStruct((MLA_HEADS, s, qk_width), BF16),
                   jax.ShapeDtypeStruct((MLA_HEADS, s // tm, V_HEAD_DIM + V_ONES_ROWS, tm), BF16)],
        compiler_params=_params(("parallel",), 2 * 2 * 512 * 4096, 2 * 3 * MLA_HEADS * tm * 256 * 2,
                                2 * tm * 1536 * 4, 8 * tm * 1024 * 4),
        name="mla_prep",
    )(proj, proj, proj, g_q, g_kv, wqn, wqp, wqr, wkn, wv, cs)


FLASH_TQ = 512
FLASH_SLOTS = 8


def _flash_kernel(qt_ref, k_ref, vt_ref, o_ref, acc_ref, s_ref, p_ref):
    tq = qt_ref.shape[2]
    n_slot, tk, _ = s_ref.shape
    tv = vt_ref.shape[3]
    n_sub = tk // tv
    n_k = vt_ref.shape[1] // n_sub
    assert n_k % n_slot == 0

    def scores(j):
        k = k_ref[0, pl.ds(pl.multiple_of(j * tk, tk), tk), :]
        return jnp.dot(k, qt_ref[0], preferred_element_type=F32)

    def accumulate(j, slot, alpha):
        pv = sum(jnp.dot(vt_ref[0, j * n_sub + u], p_ref[slot, u * tv:(u + 1) * tv, :],
                         preferred_element_type=F32) for u in range(n_sub))
        acc_ref[...] = alpha * acc_ref[...] + pv

    def step(j, slot, carry):
        m, alpha_prev, tile_max = carry
        s_next = scores(jnp.minimum(j + 1, n_k - 1))
        s_ref[(slot + 1) % n_slot] = s_next
        tile_max_next = jnp.max(s_next, axis=0, keepdims=True)
        accumulate(jnp.maximum(j - 1, 0), (slot - 1) % n_slot, alpha_prev)
        m_new = jnp.maximum(m, tile_max)
        alpha = jnp.exp2(m - m_new)
        p_ref[slot] = jnp.exp2((s_ref[slot] - m_new).astype(BF16))
        return m_new, alpha, tile_max_next

    def body(i, carry):
        for slot in range(n_slot):
            carry = step(n_slot * i + slot, slot, carry)
        return carry

    acc_ref[...] = jnp.zeros_like(acc_ref)
    p_ref[n_slot - 1] = jnp.zeros(p_ref.shape[1:], p_ref.dtype)
    s0 = scores(0)
    s_ref[0] = s0
    m0 = jnp.full((1, tq), -jnp.inf, F32)
    carry0 = (m0, jnp.ones((1, tq), F32), jnp.max(s0, axis=0, keepdims=True))
    _, alpha, _ = lax.fori_loop(0, n_k // n_slot, body, carry0)
    accumulate(n_k - 1, n_slot - 1, alpha)
    l = acc_ref[V_HEAD_DIM:V_HEAD_DIM + 1, :]
    o_ref[...] = (acc_ref[:V_HEAD_DIM, :] / l).T


def _flash(qt, k, vt):
    _, s, qk_width = k.shape
    tq, tk = min(FLASH_TQ, s), vt.shape[3]
    n_slot = min(FLASH_SLOTS, s // tk)
    return pl.pallas_call(
        _flash_kernel,
        grid=(MLA_HEADS, s // tq),
        in_specs=[pl.BlockSpec((1, qk_width, tq), lambda h, i: (h, 0, i)),
                  pl.BlockSpec((1, s, qk_width), lambda h, i: (h, 0, 0)),
                  pl.BlockSpec((1,) + vt.shape[1:], lambda h, i: (h, 0, 0, 0))],
        out_specs=pl.BlockSpec((tq, V_HEAD_DIM), lambda h, i: (i, h)),
        out_shape=jax.ShapeDtypeStruct((s, MLA_WIDTH), F32),
        scratch_shapes=[pltpu.VMEM((V_HEAD_DIM + V_ONES_ROWS, tq), F32),
                        pltpu.VMEM((n_slot, tk, tq), F32),
                        pltpu.VMEM((n_slot, tk, tq), BF16)],
        compiler_params=_params(("parallel", "arbitrary"),
                                2 * s * 256 * 2, 2 * s * V_HEAD_DIM * 2, n_slot * tk * tq * 6),
        name="mla_flash",
    )(qt, k, vt)


OUTPROJ_TM = 512


def _outproj_kernel(na_ref, mla_ref, x_ref, gna_ref, gmla_ref, gt_ref, g_ref, sc_ref, sh_ref, w_ref,
                    o_ref, hn_ref):
    na = _rms(na_ref[...], gna_ref[...]).astype(BF16)
    mla = _rms(mla_ref[...], gmla_ref[...]).astype(BF16)
    y = (jnp.dot(na, w_ref[:NA_WIDTH, :], preferred_element_type=F32)
         + jnp.dot(mla, w_ref[NA_WIDTH:, :], preferred_element_type=F32))
    x1 = x_ref[...] + gt_ref[...] * y
    o_ref[...] = x1
    hn_ref[...] = (_rms(x1, g_ref[...]) * (1.0 + sc_ref[...]) + sh_ref[...]).astype(BF16)


def _outproj(o_na, o_mla, x, g_na, g_mla, gt, g_ffn, sc, sh, w):
    s, d = x.shape
    tm = min(OUTPROJ_TM, s)
    const = lambda i: (0, 0)
    vec = pl.BlockSpec((1, d), const)
    return pl.pallas_call(
        _outproj_kernel,
        grid=(s // tm,),
        in_specs=[pl.BlockSpec((tm, NA_WIDTH), lambda i: (i, 0)),
                  pl.BlockSpec((tm, MLA_WIDTH), lambda i: (i, 0)),
                  pl.BlockSpec((tm, d), lambda i: (i, 0)),
                  pl.BlockSpec((1, NA_WIDTH), const), pl.BlockSpec((1, MLA_WIDTH), const),
                  vec, vec, vec, vec, pl.BlockSpec(w.shape, const)],
        out_specs=[pl.BlockSpec((tm, d), lambda i: (i, 0)), pl.BlockSpec((tm, d), lambda i: (i, 0))],
        out_shape=[jax.ShapeDtypeStruct((s, d), F32), jax.ShapeDtypeStruct((s, d), BF16)],
        compiler_params=_params(("parallel",), 2 * w.size * 2, 2 * 3 * tm * d * 4, 2 * tm * d * 2),
        name="outproj",
    )(o_na, o_mla, x, g_na, g_mla, gt, g_ffn, sc, sh, w)


FFN_TM = 512
FFN_TF = 512
FFN_SPLIT = 2


def _ffn_kernel(x_ref, hn_ref, gt_ref, gfin_ref, wg_ref, wu_ref, wd_ref, o_ref, acc_ref, *, n_f):
    j = pl.program_id(1)

    @pl.when(j == 0)
    def _():
        acc_ref[...] = jnp.zeros_like(acc_ref)

    h = hn_ref[...]
    half = wg_ref.shape[1] // FFN_SPLIT
    down = None
    for c in range(FFN_SPLIT):
        cols = slice(c * half, (c + 1) * half)
        a = jnp.dot(h, wg_ref[:, cols], preferred_element_type=F32)
        b = jnp.dot(h, wu_ref[:, cols], preferred_element_type=F32)
        t = (a * jax.nn.sigmoid(a) * b).astype(BF16)
        part = jnp.dot(t, wd_ref[cols, :], preferred_element_type=F32)
        down = part if down is None else down + part
    acc_ref[...] += down

    @pl.when(j == n_f - 1)
    def _():
        gt, gfin = gt_ref[...], gfin_ref[...]
        for c in range(x_ref.shape[0] // NORM_ROWS):
            rows = slice(c * NORM_ROWS, (c + 1) * NORM_ROWS)
            o_ref[rows, :] = _rms(x_ref[rows, :] + gt * acc_ref[rows, :], gfin)


def _ffn(x, hn, gt, g_final, wg, wu, wd):
    s, d = x.shape
    tm, tf = min(FFN_TM, s), FFN_TF
    n_f = D_FF // tf
    const = lambda i, j: (0, 0)
    vec = pl.BlockSpec((1, d), const)
    row = pl.BlockSpec((tm, d), lambda i, j: (i, 0))
    return pl.pallas_call(
        functools.partial(_ffn_kernel, n_f=n_f),
        grid=(s // tm, n_f),
        in_specs=[row, row, vec, vec,
                  pl.BlockSpec((d, tf), lambda i, j: (0, j)),
                  pl.BlockSpec((d, tf), lambda i, j: (0, j)),
                  pl.BlockSpec((tf, d), lambda i, j: (j, 0))],
        out_specs=row,
        out_shape=jax.ShapeDtypeStruct((s, d), F32),
        scratch_shapes=[pltpu.VMEM((tm, d), F32)],
        compiler_params=_params(("parallel", "arbitrary"),
                                4 * tm * d * 4, 2 * tm * d * 2, tm * d * 4, 2 * 3 * d * tf * 2,
                                3 * tm * tf * 4),
        name="ffn",
    )(x, hn, gt, g_final, wg, wu, wd)


def _rotate_half_cols(w):
    half = QK_ROPE_DIM // 2
    return jnp.concatenate([-w[..., half:], w[..., :half]], axis=-1)


def _rope_table(s):
    pos = jnp.arange(s, dtype=F32)
    inv_freq = ROPE_THETA ** (-jnp.arange(0, QK_ROPE_DIM, 2, dtype=F32) / QK_ROPE_DIM)
    ang = pos[:, None] * inv_freq[None, :]
    cos, sin = jnp.cos(ang), jnp.sin(ang)
    return jnp.concatenate([cos, cos, sin, sin], axis=-1)


def _layer_weights(l, w_in, w_uq, w_ukv, w_o, w_gate, w_up, w_down):
    k_pe0 = 3 * NA_WIDTH + Q_LORA_RANK + KV_LORA_RANK
    w_in_l = w_in[l]
    w_in_ext = jnp.concatenate([w_in_l, _rotate_half_cols(w_in_l[:, k_pe0:])], axis=1).astype(BF16)
    uq = w_uq[l].reshape(Q_LORA_RANK, MLA_HEADS, MLA_QK_DIM)
    wqn = uq[:, :, :QK_NOPE_DIM].reshape(Q_LORA_RANK, -1).astype(BF16)
    wqp = uq[:, :, QK_NOPE_DIM:]
    wqr = _rotate_half_cols(wqp).reshape(Q_LORA_RANK, -1).astype(BF16)
    wqp = wqp.reshape(Q_LORA_RANK, -1).astype(BF16)
    ukv = w_ukv[l].reshape(KV_LORA_RANK, MLA_HEADS, QK_NOPE_DIM + V_HEAD_DIM)
    wkn = ukv[:, :, :QK_NOPE_DIM].reshape(KV_LORA_RANK, -1).astype(BF16)
    wv = ukv[:, :, QK_NOPE_DIM:].reshape(KV_LORA_RANK, -1).astype(BF16)
    return dict(w_in=w_in_ext, wqn=wqn, wqp=wqp, wqr=wqr, wkn=wkn, wv=wv,
                w_o=w_o[l].astype(BF16), w_gate=w_gate[l].astype(BF16),
                w_up=w_up[l].astype(BF16), w_down=w_down[l].astype(BF16))


def _encoder_layer(x, mod, lw, pt, table, g_attn, g_q, g_kv, g_out_na, g_out_mla, g_ffn, g_final):
    d = D_MODEL
    sh1, sc1, gt1, sh2, sc2, gt2 = [mod[:, i * d:(i + 1) * d] for i in range(6)]
    proj = _inproj(x, g_attn, sc1, sh1, lw["w_in"])
    o_na = _na_attention(proj, pt)
    q, k, v = _mla_prep(proj, g_q, g_kv, lw["wqn"], lw["wqp"], lw["wqr"], lw["wkn"], lw["wv"], table)
    o_mla = _flash(q, k, v)
    x, hn = _outproj(o_na, o_mla, x, g_out_na, g_out_mla, gt1, g_ffn, sc2, sh2, lw["w_o"])
    return _ffn(x, hn, gt2, g_final, lw["w_gate"], lw["w_up"], lw["w_down"])


def kernel(x_prompt, x_sample, c_prompt, c_sample, w_ada, b_ada, g_attn, w_in, rpb, g_q, w_uq,
           g_kv, w_ukv, g_out_na, g_out_mla, w_o, g_ffn, w_gate, w_up, w_down, g_final):
    assert DEPTH == 1 and w_ada.shape[0] == DEPTH
    assert x_prompt.shape[0] == 1 and x_sample.shape[0] == 1
    l = 0
    c_cols = jnp.concatenate([c_prompt, c_sample], axis=0).T
    mod = _adaln(c_cols, w_ada[l], b_ada[l][None, :])
    lw = _layer_weights(l, w_in, w_uq, w_ukv, w_o, w_gate, w_up, w_down)
    pt = _na_bias(rpb[l])
    row = lambda a: a[None, :]
    table = _rope_table(max(x_prompt.shape[1], x_sample.shape[1]))
    outs = []
    for b, x in enumerate((x_prompt, x_sample)):
        y = _encoder_layer(x[0], mod[b:b + 1], lw, pt, table,
                           row(g_attn[l]), row(g_q[l]), row(g_kv[l]), row(g_out_na[l]),
                           row(g_out_mla[l]), row(g_ffn[l]), row(g_final))
        outs.append(y[None])
    return tuple(outs)
```
